```python
import math
import jax
import jax.numpy as jnp
from jax import lax
import numpy as np

D_MODEL = 2048
BATCH = 4
SEQ = 4096
DEPTH = 2

CTX_LEN = 256
GRID_W = 64
HEAD_DIM = 128
DA_WIDTH = 3 * D_MODEL // 8
NA_WIDTH = 3 * D_MODEL // 8
MLP_WIDTH = D_MODEL // 4
DA_HEADS = DA_WIDTH // HEAD_DIM
NA_HEADS = NA_WIDTH // HEAD_DIM
MLP_GROUPS = MLP_WIDTH // HEAD_DIM
DA_QK_DIM = HEAD_DIM // 2
NA_WIN_H = 8
NA_WIN_W = 16
CHUNK = 128
Q_BLOCK = 128
ROPE_BASE = 10000.0
EPS = 1e-6
IN_SIZES = (DA_WIDTH,) * 4 + (NA_WIDTH,) * 4 + (MLP_WIDTH,) * 3
IN_OFFSETS = tuple(sum(IN_SIZES[:i]) for i in range(len(IN_SIZES) + 1))
IN_WIDTH = IN_OFFSETS[-1]
MIX_WIDTH = DA_WIDTH + NA_WIDTH + MLP_WIDTH

kernel_name = "hybrid_diffattn_natten_gmlp_prefix_dit"


def rms_norm(x, g):
    xf = x.astype(jnp.float32)
    y = xf * lax.rsqrt(jnp.mean(xf * xf, axis=-1, keepdims=True) + EPS)
    return (y * g.astype(jnp.float32)).astype(x.dtype)


def layer_norm(x, g, b):
    xf = x.astype(jnp.float32)
    mu = jnp.mean(xf, axis=-1, keepdims=True)
    xc = xf - mu
    y = xc * lax.rsqrt(jnp.mean(xc * xc, axis=-1, keepdims=True) + EPS)
    return (y * g.astype(jnp.float32) + b.astype(jnp.float32)).astype(x.dtype)


def _rotate(xp, ang):
    h = xp.shape[-1] // 2
    x1, x2 = xp[..., :h], xp[..., h:]
    cos, sin = jnp.cos(ang), jnp.sin(ang)
    return jnp.concatenate([x1 * cos - x2 * sin, x2 * cos + x1 * sin], axis=-1)


def rope_2d(x, row, col):
    n = x.shape[-1] // 4
    freqs = ROPE_BASE ** (-jnp.arange(n, dtype=jnp.float32) / n)
    ar = (row.astype(jnp.float32)[:, None] * freqs)[:, None, None, :]
    ac = (col.astype(jnp.float32)[:, None] * freqs)[:, None, None, :]
    xr = _rotate(x[..., :2 * n].astype(jnp.float32), ar)
    xc = _rotate(x[..., 2 * n:].astype(jnp.float32), ac)
    return jnp.concatenate([xr, xc], axis=-1).astype(x.dtype)


def diff_attn(q, k, v, lam):
    s = jnp.einsum('bqhid,bkhid->bhiqk', q, k).astype(jnp.float32) * (q.shape[-1] ** -0.5)
    p = jax.nn.softmax(s, axis=-1)
    a = p[:, :, 0] - lam * p[:, :, 1]
    return jnp.einsum('bhqk,bkhd->bqhd', a.astype(v.dtype), v)


def softmax_attn(q, k, v):
    s = jnp.einsum('bqhd,bkhd->bhqk', q, k).astype(jnp.float32) * (q.shape[-1] ** -0.5)
    p = jax.nn.softmax(s, axis=-1)
    return jnp.einsum('bhqk,bkhd->bqhd', p.astype(v.dtype), v)


def neighbourhood_attn(q, k, v, kc, vc, rpb):
    B, S, H, d = q.shape
    rows = S // GRID_W
    kh = min(NA_WIN_H, rows)
    kw = NA_WIN_W
    qg = q.reshape(B, rows, GRID_W, H, d)
    kg = k.reshape(B, rows, GRID_W, H, d)
    vg = v.reshape(B, rows, GRID_W, H, d)
    cols = jnp.arange(GRID_W)
    cs = jnp.clip(cols - kw // 2, 0, GRID_W - kw)
    colidx = cs[:, None] + jnp.arange(kw)[None, :]
    rel_col = colidx - cols[:, None] + (NA_WIN_W - 1)
    scale = d ** -0.5

    def one_row(r):
        rs = jnp.clip(r - kh // 2, 0, rows - kh)
        qr = lax.dynamic_index_in_dim(qg, r, axis=1, keepdims=False)
        kr = lax.dynamic_slice_in_dim(kg, rs, kh, axis=1)
        vr = lax.dynamic_slice_in_dim(vg, rs, kh, axis=1)
        kwin = kr[:, :, colidx]
        vwin = vr[:, :, colidx]
        rel_row = rs + jnp.arange(kh) - r + (NA_WIN_H - 1)
        bias = rpb[:, rel_row[None, :, None], rel_col[:, None, :]]
        s_loc = (jnp.einsum('bchd,bicjhd->bhcij', qr, kwin).astype(jnp.float32) * scale
                 + bias.astype(jnp.float32)[None])
        s_loc = s_loc.reshape(B, H, GRID_W, kh * kw)
        s_ctx = jnp.einsum('bchd,bnhd->bhcn', qr, kc).astype(jnp.float32) * scale
        p = jax.nn.softmax(jnp.concatenate([s_loc, s_ctx], axis=-1), axis=-1).astype(v.dtype)
        p_loc = p[..., :kh * kw].reshape(B, H, GRID_W, kh, kw)
        p_ctx = p[..., kh * kw:]
        return (jnp.einsum('bhcij,bicjhd->bchd', p_loc, vwin)
                + jnp.einsum('bhcn,bnhd->bchd', p_ctx, vc))

    out = lax.map(one_row, jnp.arange(rows))
    return out.transpose(1, 0, 2, 3, 4).reshape(B, S, H, d)


def spatial_gating(u, v, ln_g, ln_b, w_s, b_s):
    B, N, G, c = u.shape
    vn = layer_norm(v, ln_g, ln_b).reshape(B, N // CHUNK, CHUNK, G, c)
    mixed = jnp.einsum('gpq,bnqgc->bnpgc', w_s, vn) + b_s.T[None, None, :, :, None]
    return u * mixed.reshape(B, N, G, c)


def _split_cols(p):
    return [p[..., a:b] for a, b in zip(IN_OFFSETS[:-1], IN_OFFSETS[1:])]


def _branch_out(o, z, g, factor):
    return (rms_norm(o, g) * factor).reshape(z.shape) * jax.nn.silu(z)


def _mixer(h_a, h_n, h_m, z_a, z_n, z_m, w_out, da_g, na_g, mlp_g, lam_init):
    ya = _branch_out(h_a, z_a, da_g, 1.0 - lam_init)
    yn = _branch_out(h_n, z_n, na_g, 1.0)
    ym = _branch_out(h_m, z_m, mlp_g, 1.0)
    return jnp.concatenate([ya, yn, ym], axis=-1) @ w_out


def _layer(x, xc, c, c_ctx, ada_w, ada_b, pre_g, post_g, w_in, w_out,
           lam_q1, lam_k1, lam_q2, lam_k2, da_g, na_rpb, na_g,
           mlp_ln_g, mlp_ln_b, mlp_ws, mlp_bs, mlp_g, layer_idx, ctx_out):
    B, S, _ = x.shape
    C = xc.shape[1]
    lam_init = 0.8 - 0.6 * math.exp(-0.3 * layer_idx)
    lam = (jnp.exp(jnp.sum(lam_q1.astype(jnp.float32) * lam_k1.astype(jnp.float32)))
           - jnp.exp(jnp.sum(lam_q2.astype(jnp.float32) * lam_k2.astype(jnp.float32)))
           + lam_init)

    shift, scale, gate = jnp.split(jax.nn.silu(c) @ ada_w + ada_b, 3, axis=-1)
    shift_c, scale_c, gate_c = jnp.split(jax.nn.silu(c_ctx) @ ada_w + ada_b, 3, axis=-1)
    h = rms_norm(x, pre_g) * (1.0 + scale[:, None, :]) + shift[:, None, :]
    hc = rms_norm(xc, pre_g) * (1.0 + scale_c) + shift_c

    aq, ak, av, az, nq, nk, nv, nz, mu, mv, mz = _split_cols(h @ w_in)
    if ctx_out:
        caq, cak, cav, caz, cnq, cnk, cnv, cnz, cmu, cmv, cmz = _split_cols(hc @ w_in)
    else:
        cak, cav = jnp.split(hc @ w_in[:, IN_OFFSETS[1]:IN_OFFSETS[3]], 2, axis=-1)
        cnk, cnv = jnp.split(hc @ w_in[:, IN_OFFSETS[5]:IN_OFFSETS[7]], 2, axis=-1)

    pos = jnp.arange(S)
    row, col = pos // GRID_W, pos % GRID_W

    dq = rope_2d(aq.reshape(B, S, DA_HEADS, 2, DA_QK_DIM), row, col)
    dk = rope_2d(ak.reshape(B, S, DA_HEADS, 2, DA_QK_DIM), row, col)
    dv = av.reshape(B, S, DA_HEADS, HEAD_DIM)
    dkc = cak.reshape(B, C, DA_HEADS, 2, DA_QK_DIM)
    dvc = cav.reshape(B, C, DA_HEADS, HEAD_DIM)
    k_all = jnp.concatenate([dk, dkc], axis=1)
    v_all = jnp.concatenate([dv, dvc], axis=1)
    qb = dq.reshape(B, S // Q_BLOCK, Q_BLOCK, DA_HEADS, 2, DA_QK_DIM).swapaxes(0, 1)
    oa = lax.map(lambda qi: diff_attn(qi, k_all, v_all, lam), qb)
    oa = oa.swapaxes(0, 1).reshape(B, S, DA_HEADS, HEAD_DIM)

    nkc = cnk.reshape(B, C, NA_HEADS, HEAD_DIM)
    nvc = cnv.reshape(B, C, NA_HEADS, HEAD_DIM)
    on = neighbourhood_attn(nq.reshape(B, S, NA_HEADS, HEAD_DIM), nk.reshape(B, S, NA_HEADS, HEAD_DIM),
                            nv.reshape(B, S, NA_HEADS, HEAD_DIM), nkc, nvc, na_rpb)

    om = spatial_gating(jax.nn.gelu(mu).reshape(B, S, MLP_GROUPS, HEAD_DIM),
                        jax.nn.gelu(mv).reshape(B, S, MLP_GROUPS, HEAD_DIM),
                        mlp_ln_g, mlp_ln_b, mlp_ws, mlp_bs)

    y = _mixer(oa, on, om, az, nz, mz, w_out, da_g, na_g, mlp_g, lam_init)
    x = x + gate[:, None, :] * rms_norm(y, post_g)

    if ctx_out:
        oca = diff_attn(caq.reshape(B, C, DA_HEADS, 2, DA_QK_DIM), dkc, dvc, lam)
        ocn = softmax_attn(cnq.reshape(B, C, NA_HEADS, HEAD_DIM), nkc, nvc)
        ocm = spatial_gating(jax.nn.gelu(cmu).reshape(B, C, MLP_GROUPS, HEAD_DIM),
                             jax.nn.gelu(cmv).reshape(B, C, MLP_GROUPS, HEAD_DIM),
                             mlp_ln_g, mlp_ln_b, mlp_ws, mlp_bs)
        yc = _mixer(oca, ocn, ocm, caz, cnz, cmz, w_out, da_g, na_g, mlp_g, lam_init)
        xc = xc + gate_c * rms_norm(yc, post_g)
    return x, xc


def setup_inputs(seed: int = 0) -> dict:
    key = jax.random.key(seed)
    ks = jax.random.split(key, 24)
    n = jax.random.normal
    f = jnp.float32
    L, D = DEPTH, D_MODEL
    return {
        "x": n(ks[0], (BATCH, SEQ, D), f),
        "c": n(ks[1], (BATCH, D), f),
        "ctx": n(ks[2], (BATCH, CTX_LEN, D), f),
        "c_ctx": n(ks[3], (D,), f),
        "ada_w": n(ks[4], (L, D, 3 * D), f) * (0.5 * D ** -0.5),
        "ada_b": n(ks[5], (L, 3 * D), f) * 0.01,
        "pre_g": 1.0 + 0.05 * n(ks[6], (L, D), f),
        "post_g": 1.0 + 0.05 * n(ks[7], (L, D), f),
        "w_in": n(ks[8], (L, D, IN_WIDTH), f) * D ** -0.5,
        "w_out": n(ks[9], (L, MIX_WIDTH, D), f) * MIX_WIDTH ** -0.5,
        "lam_q1": 0.1 * n(ks[10], (L, DA_QK_DIM), f),
        "lam_k1": 0.1 * n(ks[11], (L, DA_QK_DIM), f),
        "lam_q2": 0.1 * n(ks[12], (L, DA_QK_DIM), f),
        "lam_k2": 0.1 * n(ks[13], (L, DA_QK_DIM), f),
        "da_g": 1.0 + 0.05 * n(ks[14], (L, HEAD_DIM), f),
        "na_rpb": 0.1 * n(ks[15], (L, NA_HEADS, 2 * NA_WIN_H - 1, 2 * NA_WIN_W - 1), f),
        "na_g": 1.0 + 0.05 * n(ks[16], (L, NA_HEADS, HEAD_DIM), f),
        "mlp_ln_g": 1.0 + 0.05 * n(ks[17], (L, MLP_GROUPS, HEAD_DIM), f),
        "mlp_ln_b": 0.02 * n(ks[18], (L, MLP_GROUPS, HEAD_DIM), f),
        "mlp_ws": n(ks[19], (L, MLP_GROUPS, CHUNK, CHUNK), f) * CHUNK ** -0.5,
        "mlp_bs": 1.0 + 0.05 * n(ks[20], (L, MLP_GROUPS, CHUNK), f),
        "mlp_g": 1.0 + 0.05 * n(ks[21], (L, MLP_GROUPS, HEAD_DIM), f),
    }


def reference(x, c, ctx, c_ctx, ada_w, ada_b, pre_g, post_g, w_in, w_out,
              lam_q1, lam_k1, lam_q2, lam_k2, da_g, na_rpb, na_g,
              mlp_ln_g, mlp_ln_b, mlp_ws, mlp_bs, mlp_g):
    xc = ctx
    for l in range(DEPTH):
        x, xc = _layer(x, xc, c, c_ctx, ada_w[l], ada_b[l], pre_g[l], post_g[l], w_in[l], w_out[l],
                       lam_q1[l], lam_k1[l], lam_q2[l], lam_k2[l], da_g[l], na_rpb[l], na_g[l],
                       mlp_ln_g[l], mlp_ln_b[l], mlp_ws[l], mlp_bs[l], mlp_g[l],
                       l, l < DEPTH - 1)
    return x
```

```python
import functools
import math

import numpy as np
import jax
import jax.numpy as jnp
from jax import lax
from jax.experimental import pallas as pl
from jax.experimental.pallas import tpu as pltpu

D_MODEL = 2048
DEPTH = 2
GRID_W = 64
HEAD_DIM = 128
DA_HEADS = 6
NA_HEADS = 6
MLP_GROUPS = 4
DA_WIDTH = DA_HEADS * HEAD_DIM
NA_WIDTH = NA_HEADS * HEAD_DIM
MLP_WIDTH = MLP_GROUPS * HEAD_DIM
DA_QK_DIM = HEAD_DIM // 2
NA_WIN_H = 8
NA_WIN_W = 16
CHUNK = 128
ROPE_BASE = 10000.0
EPS = 1e-6
IN_WIDTH = 4 * DA_WIDTH + 4 * NA_WIDTH + 3 * MLP_WIDTH
MIX_WIDTH = DA_WIDTH + NA_WIDTH + MLP_WIDTH

COL_AQ, COL_AK, COL_AV, COL_AZ = 0, 6, 12, 18
COL_BQ, COL_BK, COL_BV, COL_BZ = 24, 30, 36, 42
COL_CU, COL_CV, COL_CZ = 12, 13, 14

IN_TILE_N = 2 * DA_WIDTH
DA_SCALE = DA_QK_DIM ** -0.5
NA_SCALE = HEAD_DIM ** -0.5

NA_BLOCK_ROWS = 4
NA_KEY_ROWS = NA_BLOCK_ROWS + NA_WIN_H
NA_Q = NA_BLOCK_ROWS * GRID_W
NA_K = NA_KEY_ROWS * GRID_W
MASK_VALUE = -1e30

VMEM_LIMIT_BYTES = 56 * 1024 * 1024

_BF16 = jnp.bfloat16
_F32 = jnp.float32


def _silu(x):
    return x * (1.0 / (1.0 + jnp.exp(-x)))


def _gelu_tanh(x):
    c = math.sqrt(2.0 / math.pi)
    return 0.5 * x * (1.0 + jnp.tanh(c * (x + 0.044715 * (x * x * x))))


def _rms(x, g):
    return x * lax.rsqrt(jnp.mean(x * x, axis=-1, keepdims=True) + EPS) * g


def _dot_nt(a, b):
    return lax.dot_general(a, b, (((1,), (1,)), ((), ())), preferred_element_type=_F32)


def _params(*sem):
    return pltpu.CompilerParams(dimension_semantics=sem, vmem_limit_bytes=VMEM_LIMIT_BYTES)


def _ada_kernel(c_ref, w_ref, b_ref, o_ref):
    sc = _silu(c_ref[...]).astype(_BF16)
    o_ref[...] = jnp.dot(sc, w_ref[...].astype(_BF16), preferred_element_type=_F32) + b_ref[...]


def _ada(cc, ada_w, ada_b):
    L, D, N = ada_w.shape
    R = cc.shape[0]
    tn = 768
    return pl.pallas_call(
        _ada_kernel,
        grid=(L, N // tn),
        in_specs=[
            pl.BlockSpec((R, D), lambda l, j: (0, 0)),
            pl.BlockSpec((None, D, tn), lambda l, j: (l, 0, j)),
            pl.BlockSpec((None, 1, tn), lambda l, j: (l, 0, j)),
        ],
        out_specs=pl.BlockSpec((None, R, tn), lambda l, j: (l, 0, j)),
        out_shape=jax.ShapeDtypeStruct((L, R, N), _F32),
        name="ada_mod",
        compiler_params=_params("parallel", "parallel"),
    )(cc, ada_w, ada_b.reshape(L, 1, N))


def _in_kernel(x_ref, g_ref, scale_ref, shift_ref, w_ref, cos_ref, sa_ref, sb_ref, o_ref, h_ref):
    j = pl.program_id(1)

    @pl.when(j == 0)
    def _():
        x = x_ref[...]
        h = _rms(x, g_ref[...]) * (1.0 + scale_ref[...]) + shift_ref[...]
        h_ref[...] = h.astype(_BF16)

    acc = jnp.dot(h_ref[...], w_ref[...], preferred_element_type=_F32)

    @pl.when(j == 0)
    def _():
        cos, sa, sb = cos_ref[...], sa_ref[...], sb_ref[...]
        for hh in range(2 * DA_HEADS):
            xs = acc[:, hh * HEAD_DIM:(hh + 1) * HEAD_DIM]
            r = xs * cos + pltpu.roll(xs, HEAD_DIM - 16, 1) * sa + pltpu.roll(xs, 16, 1) * sb
            if hh < DA_HEADS:
                r = r * DA_SCALE
            o_ref[:, hh * HEAD_DIM:(hh + 1) * HEAD_DIM] = r.astype(_BF16)

    @pl.when(j == 2)
    def _():
        o_ref[:, :NA_WIDTH] = (acc[:, :NA_WIDTH] * NA_SCALE).astype(_BF16)
        o_ref[:, NA_WIDTH:] = acc[:, NA_WIDTH:].astype(_BF16)

    @pl.when(j == 4)
    def _():
        o_ref[:, :2 * MLP_WIDTH] = _gelu_tanh(acc[:, :2 * MLP_WIDTH]).astype(_BF16)
        o_ref[:, 2 * MLP_WIDTH:] = acc[:, 2 * MLP_WIDTH:].astype(_BF16)

    @pl.when((j == 1) | (j == 3))
    def _():
        o_ref[...] = acc.astype(_BF16)


def _in_proj(x2d, pre_g, scale, shift, w_bf16, cos, sa, sb, *, tm, seq):
    T, D = x2d.shape
    nb = scale.shape[0]
    tiles_per_mod = (T // nb) // tm
    tiles_per_seq = seq // tm
    return pl.pallas_call(
        _in_kernel,
        grid=(T // tm, IN_WIDTH // IN_TILE_N),
        in_specs=[
            pl.BlockSpec((tm, D), lambda i, j: (i, 0)),
            pl.BlockSpec((1, D), lambda i, j: (0, 0)),
            pl.BlockSpec((None, 1, D), lambda i, j: (i // tiles_per_mod, 0, 0)),
            pl.BlockSpec((None, 1, D), lambda i, j: (i // tiles_per_mod, 0, 0)),
            pl.BlockSpec((D, IN_TILE_N), lambda i, j: (0, j)),
            pl.BlockSpec((tm, HEAD_DIM), lambda i, j: (i % tiles_per_seq, 0)),
            pl.BlockSpec((tm, HEAD_DIM), lambda i, j: (i % tiles_per_seq, 0)),
            pl.BlockSpec((tm, HEAD_DIM), lambda i, j: (i % tiles_per_seq, 0)),
        ],
        out_specs=pl.BlockSpec((tm, IN_TILE_N), lambda i, j: (i, j)),
        out_shape=jax.ShapeDtypeStruct((T, IN_WIDTH), _BF16),
        scratch_shapes=[pltpu.VMEM((tm, D), _BF16)],
        name="in_proj",
        compiler_params=_params("parallel", "arbitrary"),
    )(x2d, pre_g.reshape(1, D), scale, shift, w_bf16, cos, sa, sb)


def _softmax_pv(qq, sources):
    scores = []
    for k, _, bias in sources:
        s = _dot_nt(qq, k)
        if bias is not None:
            s = s + bias
        scores.append(s)
    m = jnp.max(scores[0], axis=-1, keepdims=True)
    for s in scores[1:]:
        m = jnp.maximum(m, jnp.max(s, axis=-1, keepdims=True))
    l = None
    o = None
    for s, (_, v, _) in zip(scores, sources):
        p = jnp.exp(s - m)
        ps = jnp.sum(p, axis=-1, keepdims=True)
        pv = jnp.dot(p.astype(_BF16), v, preferred_element_type=_F32)
        l = ps if l is None else l + ps
        o = pv if o is None else o + pv
    return o / l


def _branch_out(o, g, factor, z_ref, o_ref):
    z = z_ref[...].astype(_F32)
    o_ref[...] = (_rms(o, g) * factor * _silu(z)).astype(_BF16)


def _da_kernel(*refs, n_src, lam_init):
    q_ref, z_ref = refs[0], refs[1]
    kv_refs = refs[2:2 + 2 * n_src]
    lam_ref, g_ref, o_ref = refs[2 + 2 * n_src:]

    q = q_ref[...]
    tq = q.shape[0]
    lane = lax.broadcasted_iota(jnp.int32, q.shape, 1)
    zero = jnp.zeros_like(q)
    qq = jnp.concatenate([jnp.where(lane < DA_QK_DIM, q, zero),
                          jnp.where(lane >= DA_QK_DIM, q, zero)], axis=0)
    sources = [(kv_refs[2 * i][...], kv_refs[2 * i + 1][...], None) for i in range(n_src)]
    o = _softmax_pv(qq, sources)

    lp = lam_ref[...]
    lam = (jnp.exp(jnp.sum(lp[0:1] * lp[1:2], axis=-1, keepdims=True))
           - jnp.exp(jnp.sum(lp[2:3] * lp[3:4], axis=-1, keepdims=True)) + lam_init)
    oa = o[:tq] - lam * o[tq:]
    _branch_out(oa, g_ref[...], 1.0 - lam_init, z_ref, o_ref)


def _diff_attn(q_proj, kv_projs, lam_params, da_g, *, tq, lam_init):
    B, Sq, _ = q_proj.shape
    in_specs = [
        pl.BlockSpec((None, tq, HEAD_DIM), lambda b, h, i: (b, i, COL_AQ + h)),
        pl.BlockSpec((None, tq, HEAD_DIM), lambda b, h, i: (b, i, COL_AZ + h)),
    ]
    args = [q_proj, q_proj]
    for kp in kv_projs:
        sk = kp.shape[1]
        in_specs.append(pl.BlockSpec((None, sk, HEAD_DIM), lambda b, h, i: (b, 0, COL_AK + h)))
        in_specs.append(pl.BlockSpec((None, sk, HEAD_DIM), lambda b, h, i: (b, 0, COL_AV + h)))
        args += [kp, kp]
    in_specs += [
        pl.BlockSpec((4, DA_QK_DIM), lambda b, h, i: (0, 0)),
        pl.BlockSpec((1, HEAD_DIM), lambda b, h, i: (0, 0)),
    ]
    args += [lam_params, da_g.reshape(1, HEAD_DIM)]
    return pl.pallas_call(
        functools.partial(_da_kernel, n_src=len(kv_projs), lam_init=lam_init),
        grid=(B, DA_HEADS, Sq // tq),
        in_specs=in_specs,
        out_specs=pl.BlockSpec((None, tq, HEAD_DIM), lambda b, h, i: (b, i, h)),
        out_shape=jax.ShapeDtypeStruct((B, Sq, DA_WIDTH), _BF16),
        name="diff_attn",
        compiler_params=_params("parallel", "parallel", "parallel"),
    )(*args)


def _na_kernel(q_ref, z_ref, k_ref, v_ref, kc_ref, vc_ref, bias_ref, g_ref, o_ref):
    rb = pl.program_id(2)
    rows = k_ref.shape[0] // GRID_W
    kr0 = jnp.clip(rb * NA_BLOCK_ROWS - NA_WIN_H // 2, 0, rows - NA_KEY_ROWS)
    start = pl.multiple_of(kr0 * GRID_W, GRID_W)
    k_win = k_ref[pl.ds(start, NA_K), :]
    v_win = v_ref[pl.ds(start, NA_K), :]
    o = _softmax_pv(q_ref[...], [(k_win, v_win, bias_ref[...]), (kc_ref[...], vc_ref[...], None)])
    _branch_out(o, g_ref[...], 1.0, z_ref, o_ref)


def _na_bias_tables(rpb, rows):
    qi = np.arange(NA_Q)
    kj = np.arange(NA_K)
    tabs = []
    for r0, kr0 in ((0, 0), (2 * NA_BLOCK_ROWS, 2 * NA_BLOCK_ROWS - NA_WIN_H // 2),
                    (rows - NA_BLOCK_ROWS, rows - NA_KEY_ROWS)):
        r = (r0 + qi // GRID_W)[:, None]
        c = (qi % GRID_W)[:, None]
        kr = (kr0 + kj // GRID_W)[None, :]
        kc = (kj % GRID_W)[None, :]
        rs = np.clip(r - NA_WIN_H // 2, 0, rows - NA_WIN_H)
        cs = np.clip(c - NA_WIN_W // 2, 0, GRID_W - NA_WIN_W)
        valid = (kr >= rs) & (kr < rs + NA_WIN_H) & (kc >= cs) & (kc < cs + NA_WIN_W)
        rel_r = np.clip(kr - r + NA_WIN_H - 1, 0, 2 * NA_WIN_H - 2)
        rel_c = np.clip(kc - c + NA_WIN_W - 1, 0, 2 * NA_WIN_W - 2)
        b = rpb[:, rel_r, rel_c].astype(_F32)
        tabs.append(jnp.where(valid[None], b, MASK_VALUE))
    return jnp.stack(tabs)


def _nbr_attn(proj, proj_ctx, bias_tabs, na_g):
    B, S, _ = proj.shape
    C = proj_ctx.shape[1]
    n_blocks = S // NA_Q

    def bias_idx(b, h, i):
        return (jnp.where(i == 0, 0, jnp.where(i == n_blocks - 1, 2, 1)), h, 0, 0)

    return pl.pallas_call(
        _na_kernel,
        grid=(B, NA_HEADS, n_blocks),
        in_specs=[
            pl.BlockSpec((None, NA_Q, HEAD_DIM), lambda b, h, i: (b, i, COL_BQ + h)),
            pl.BlockSpec((None, NA_Q, HEAD_DIM), lambda b, h, i: (b, i, COL_BZ + h)),
            pl.BlockSpec((None, S, HEAD_DIM), lambda b, h, i: (b, 0, COL_BK + h)),
            pl.BlockSpec((None, S, HEAD_DIM), lambda b, h, i: (b, 0, COL_BV + h)),
            pl.BlockSpec((None, C, HEAD_DIM), lambda b, h, i: (b, 0, COL_BK + h)),
            pl.BlockSpec((None, C, HEAD_DIM), lambda b, h, i: (b, 0, COL_BV + h)),
            pl.BlockSpec((None, None, NA_Q, NA_K), bias_idx),
            pl.BlockSpec((None, 1, HEAD_DIM), lambda b, h, i: (h, 0, 0)),
        ],
        out_specs=pl.BlockSpec((None, NA_Q, HEAD_DIM), lambda b, h, i: (b, i, h)),
        out_shape=jax.ShapeDtypeStruct((B, S, NA_WIDTH), _BF16),
        name="nbr_attn",
        compiler_params=_params("parallel", "parallel", "parallel"),
    )(proj, proj, proj, proj, proj_ctx, proj_ctx, bias_tabs, na_g.reshape(NA_HEADS, 1, HEAD_DIM))


def _ctx_attn_kernel(q_ref, z_ref, k_ref, v_ref, g_ref, o_ref):
    o = _softmax_pv(q_ref[...], [(k_ref[...], v_ref[...], None)])
    _branch_out(o, g_ref[...], 1.0, z_ref, o_ref)


def _ctx_attn(proj_ctx, na_g):
    B, C, _ = proj_ctx.shape
    return pl.pallas_call(
        _ctx_attn_kernel,
        grid=(B, NA_HEADS),
        in_specs=[
            pl.BlockSpec((None, C, HEAD_DIM), lambda b, h: (b, 0, COL_BQ + h)),
            pl.BlockSpec((None, C, HEAD_DIM), lambda b, h: (b, 0, COL_BZ + h)),
            pl.BlockSpec((None, C, HEAD_DIM), lambda b, h: (b, 0, COL_BK + h)),
            pl.BlockSpec((None, C, HEAD_DIM), lambda b, h: (b, 0, COL_BV + h)),
            pl.BlockSpec((None, 1, HEAD_DIM), lambda b, h: (h, 0, 0)),
        ],
        out_specs=pl.BlockSpec((None, C, HEAD_DIM), lambda b, h: (b, 0, h)),
        out_shape=jax.ShapeDtypeStruct((B, C, NA_WIDTH), _BF16),
        name="ctx_attn",
        compiler_params=_params("parallel", "parallel"),
    )(proj_ctx, proj_ctx, proj_ctx, proj_ctx, na_g.reshape(NA_HEADS, 1, HEAD_DIM))


def _mlp_kernel(u_ref, v_ref, z_ref, lng_ref, lnb_ref, ws_ref, bs_ref, g_ref, o_ref):
    n_chunks = u_ref.shape[0] // CHUNK
    for g in range(MLP_GROUPS):
        cols = slice(g * HEAD_DIM, (g + 1) * HEAD_DIM)
        w = ws_ref[g].astype(_BF16)
        bs = bs_ref[g]
        ln_g, ln_b, out_g = lng_ref[:, cols], lnb_ref[:, cols], g_ref[:, cols]
        for c in range(n_chunks):
            rows = slice(c * CHUNK, (c + 1) * CHUNK)
            v = v_ref[rows, cols].astype(_F32)
            vc = v - jnp.mean(v, axis=-1, keepdims=True)
            vn = vc * lax.rsqrt(jnp.mean(vc * vc, axis=-1, keepdims=True) + EPS) * ln_g + ln_b
            mixed = jnp.dot(w, vn.astype(_BF16), preferred_element_type=_F32) + bs
            om = u_ref[rows, cols].astype(_F32) * mixed
            z = z_ref[rows, cols].astype(_F32)
            o_ref[rows, cols] = (_rms(om, out_g) * _silu(z)).astype(_BF16)


def _gmlp(proj, ln_g, ln_b, ws, bs, out_g, *, rows_per_step):
    B, S, _ = proj.shape
    return pl.pallas_call(
        _mlp_kernel,
        grid=(B, S // rows_per_step),
        in_specs=[
            pl.BlockSpec((None, rows_per_step, MLP_WIDTH), lambda b, i: (b, i, COL_CU)),
            pl.BlockSpec((None, rows_per_step, MLP_WIDTH), lambda b, i: (b, i, COL_CV)),
            pl.BlockSpec((None, rows_per_step, MLP_WIDTH), lambda b, i: (b, i, COL_CZ)),
            pl.BlockSpec((1, MLP_WIDTH), lambda b, i: (0, 0)),
            pl.BlockSpec((1, MLP_WIDTH), lambda b, i: (0, 0)),
            pl.BlockSpec((MLP_GROUPS, CHUNK, CHUNK), lambda b, i: (0, 0, 0)),
            pl.BlockSpec((MLP_GROUPS, CHUNK, 1), lambda b, i: (0, 0, 0)),
            pl.BlockSpec((1, MLP_WIDTH), lambda b, i: (0, 0)),
        ],
        out_specs=pl.BlockSpec((None, rows_per_step, MLP_WIDTH), lambda b, i: (b, i, 0)),
        out_shape=jax.ShapeDtypeStruct((B, S, MLP_WIDTH), _BF16),
        name="gmlp",
        compiler_params=_params("parallel", "parallel"),
    )(proj, proj, proj, ln_g.reshape(1, MLP_WIDTH), ln_b.reshape(1, MLP_WIDTH), ws,
      bs.reshape(MLP_GROUPS, CHUNK, 1), out_g.reshape(1, MLP_WIDTH))


def _out_kernel(ya_ref, yn_ref, ym_ref, w_ref, x_ref, gate_ref, g_ref, o_ref):
    y = jnp.dot(ya_ref[...], w_ref[:DA_WIDTH, :], preferred_element_type=_F32)
    y = y + jnp.dot(yn_ref[...], w_ref[DA_WIDTH:DA_WIDTH + NA_WIDTH, :], preferred_element_type=_F32)
    y = y + jnp.dot(ym_ref[...], w_ref[DA_WIDTH + NA_WIDTH:, :], preferred_element_type=_F32)
    o_ref[...] = x_ref[...] + gate_ref[...] * _rms(y, g_ref[...])


def _out_proj(ya, yn, ym, w_bf16, x2d, gate, post_g, *, tm):
    T, D = x2d.shape
    nb = gate.shape[0]
    tiles_per_mod = (T // nb) // tm
    return pl.pallas_call(
        _out_kernel,
        grid=(T // tm,),
        in_specs=[
            pl.BlockSpec((tm, DA_WIDTH), lambda i: (i, 0)),
            pl.BlockSpec((tm, NA_WIDTH), lambda i: (i, 0)),
            pl.BlockSpec((tm, MLP_WIDTH), lambda i: (i, 0)),
            pl.BlockSpec((MIX_WIDTH, D), lambda i: (0, 0)),
            pl.BlockSpec((tm, D), lambda i: (i, 0)),
            pl.BlockSpec((None, 1, D), lambda i: (i // tiles_per_mod, 0, 0)),
            pl.BlockSpec((1, D), lambda i: (0, 0)),
        ],
        out_specs=pl.BlockSpec((tm, D), lambda i: (i, 0)),
        out_shape=jax.ShapeDtypeStruct((T, D), _F32),
        name="out_proj",
        compiler_params=_params("parallel"),
    )(ya.reshape(T, DA_WIDTH), yn.reshape(T, NA_WIDTH), ym.reshape(T, MLP_WIDTH), w_bf16, x2d,
      gate, post_g.reshape(1, D))


def _rope_tables(seq):
    n = DA_QK_DIM // 4
    freqs = ROPE_BASE ** (-jnp.arange(n, dtype=_F32) / n)
    pos = jnp.arange(seq)
    row = (pos // GRID_W).astype(_F32)[:, None] * freqs
    col = (pos % GRID_W).astype(_F32)[:, None] * freqs
    zeros = jnp.zeros((seq, n), _F32)

    def one_map(fn_first, fn_second):
        return jnp.concatenate([fn_first(row), fn_second(row), fn_first(col), fn_second(col)], axis=-1)

    cos = one_map(jnp.cos, jnp.cos)
    sa = one_map(lambda a: -jnp.sin(a), lambda a: zeros)
    sb = one_map(lambda a: zeros, jnp.sin)
    return tuple(jnp.concatenate([t, t], axis=-1) for t in (cos, sa, sb))


def _identity_rope_tables(seq):
    return (jnp.ones((seq, HEAD_DIM), _F32), jnp.zeros((seq, HEAD_DIM), _F32),
            jnp.zeros((seq, HEAD_DIM), _F32))


def kernel(x, c, ctx, c_ctx, ada_w, ada_b, pre_g, post_g, w_in, w_out, lam_q1, lam_k1, lam_q2, lam_k2,
           da_g, na_rpb, na_g, mlp_ln_g, mlp_ln_b, mlp_ws, mlp_bs, mlp_g):
    B, S, D = x.shape
    C = ctx.shape[1]
    L = ada_w.shape[0]
    assert S % NA_Q == 0 and S // GRID_W >= NA_KEY_ROWS and C % CHUNK == 0

    cc = jnp.concatenate([c, c_ctx[None], jnp.zeros((8 - B - 1, D), _F32)], axis=0)
    mod = _ada(cc, ada_w, ada_b)

    rope = _rope_tables(S)
    rope_id = _identity_rope_tables(C)
    x2d = x.reshape(B * S, D)
    xc2d = ctx.reshape(B * C, D)

    for l in range(L):
        lam_init = 0.8 - 0.6 * math.exp(-0.3 * l)
        ctx_out = l < L - 1
        shift, scale, gate = (mod[l, :B, i * D:(i + 1) * D].reshape(B, 1, D) for i in range(3))
        shift_c, scale_c, gate_c = (mod[l, B:B + 1, i * D:(i + 1) * D].reshape(1, 1, D) for i in range(3))
        w_in_b = w_in[l].astype(_BF16)
        w_out_b = w_out[l].astype(_BF16)
        lam_params = jnp.stack([lam_q1[l], lam_k1[l], lam_q2[l], lam_k2[l]])

        proj = _in_proj(x2d, pre_g[l], scale, shift, w_in_b, *rope, tm=512, seq=S).reshape(B, S, IN_WIDTH)
        proj_c = _in_proj(xc2d, pre_g[l], scale_c, shift_c, w_in_b, *rope_id, tm=C, seq=C).reshape(B, C, IN_WIDTH)

        ya = _diff_attn(proj, [proj, proj_c], lam_params, da_g[l], tq=128, lam_init=lam_init)
        yn = _nbr_attn(proj, proj_c, _na_bias_tables(na_rpb[l], S // GRID_W), na_g[l])
        ym = _gmlp(proj, mlp_ln_g[l], mlp_ln_b[l], mlp_ws[l], mlp_bs[l], mlp_g[l], rows_per_step=512)
        x2d_new = _out_proj(ya, yn, ym, w_out_b, x2d, gate, post_g[l], tm=256)

        if ctx_out:
            yca = _diff_attn(proj_c, [proj_c], lam_params, da_g[l], tq=C, lam_init=lam_init)
            ycn = _ctx_attn(proj_c, na_g[l])
            ycm = _gmlp(proj_c, mlp_ln_g[l], mlp_ln_b[l], mlp_ws[l], mlp_bs[l], mlp_g[l], rows_per_step=C)
            xc2d = _out_proj(yca, ycn, ycm, w_out_b, xc2d, gate_c, post_g[l], tm=C)
        x2d = x2d_new

    return x2d.reshape(B, S, D)
```

```python
import functools
import math

import numpy as np
import jax
import jax.numpy as jnp
from jax import lax
from jax.experimental import pallas as pl
from jax.experimental.pallas import tpu as pltpu

D_MODEL = 2048
DEPTH = 2
GRID_W = 64
HEAD_DIM = 128
DA_HEADS = 6
NA_HEADS = 6
MLP_GROUPS = 4
DA_WIDTH = DA_HEADS * HEAD_DIM
NA_WIDTH = NA_HEADS * HEAD_DIM
MLP_WIDTH = MLP_GROUPS * HEAD_DIM
DA_QK_DIM = HEAD_DIM // 2
NA_WIN_H = 8
NA_WIN_W = 16
CHUNK = 128
ROPE_BASE = 10000.0
EPS = 1e-6
IN_WIDTH = 4 * DA_WIDTH + 4 * NA_WIDTH + 3 * MLP_WIDTH
MIX_WIDTH = DA_WIDTH + NA_WIDTH + MLP_WIDTH

COL_AQ, COL_AK, COL_AV, COL_AZ = 0, 6, 12, 18
COL_BQ, COL_BK, COL_BV, COL_BZ = 24, 30, 36, 42
COL_CU, COL_CV, COL_CZ = 12, 13, 14

IN_TILE_N = 2 * DA_WIDTH
DA_SCALE = DA_QK_DIM ** -0.5
NA_SCALE = HEAD_DIM ** -0.5

NA_BLOCK_ROWS = 4
NA_KEY_ROWS = NA_BLOCK_ROWS + NA_WIN_H
NA_Q = NA_BLOCK_ROWS * GRID_W
NA_K = NA_KEY_ROWS * GRID_W
MASK_VALUE = -1e30

VMEM_LIMIT_BYTES = 56 * 1024 * 1024

_BF16 = jnp.bfloat16
_F32 = jnp.float32


def _silu(x):
    return x * (1.0 / (1.0 + jnp.exp(-x)))


def _gelu_tanh(x):
    c = math.sqrt(2.0 / math.pi)
    return 0.5 * x * (1.0 + jnp.tanh(c * (x + 0.044715 * (x * x * x))))


def _rms(x, g):
    return x * lax.rsqrt(jnp.mean(x * x, axis=-1, keepdims=True) + EPS) * g


def _dot_nt(a, b):
    return lax.dot_general(a, b, (((1,), (1,)), ((), ())), preferred_element_type=_F32)


def _params(*sem):
    return pltpu.CompilerParams(dimension_semantics=sem, vmem_limit_bytes=VMEM_LIMIT_BYTES)


def _ada_kernel(c_ref, w_ref, b_ref, o_ref):
    sc = _silu(c_ref[...]).astype(_BF16)
    o_ref[...] = jnp.dot(sc, w_ref[...].astype(_BF16), preferred_element_type=_F32) + b_ref[...]


def _ada(cc, ada_w, ada_b):
    L, D, N = ada_w.shape
    R = cc.shape[0]
    tn = 768
    return pl.pallas_call(
        _ada_kernel,
        grid=(L, N // tn),
        in_specs=[
            pl.BlockSpec((R, D), lambda l, j: (0, 0)),
            pl.BlockSpec((None, D, tn), lambda l, j: (l, 0, j)),
            pl.BlockSpec((None, 1, tn), lambda l, j: (l, 0, j)),
        ],
        out_specs=pl.BlockSpec((None, R, tn), lambda l, j: (l, 0, j)),
        out_shape=jax.ShapeDtypeStruct((L, R, N), _F32),
        name="ada_mod",
        compiler_params=_params("parallel", "parallel"),
    )(cc, ada_w, ada_b.reshape(L, 1, N))


def _in_kernel(x_ref, g_ref, scale_ref, shift_ref, w_ref, cos_ref, sa_ref, sb_ref, o_ref, h_ref):
    j = pl.program_id(1)

    @pl.when(j == 0)
    def _():
        x = x_ref[...]
        h = _rms(x, g_ref[...]) * (1.0 + scale_ref[...]) + shift_ref[...]
        h_ref[...] = h.astype(_BF16)

    acc = jnp.dot(h_ref[...], w_ref[...], preferred_element_type=_F32)

    @pl.when(j == 0)
    def _():
        cos, sa, sb = cos_ref[...], sa_ref[...], sb_ref[...]
        for hh in range(2 * DA_HEADS):
            xs = acc[:, hh * HEAD_DIM:(hh + 1) * HEAD_DIM]
            r = xs * cos + pltpu.roll(xs, HEAD_DIM - 16, 1) * sa + pltpu.roll(xs, 16, 1) * sb
            if hh < DA_HEADS:
                r = r * DA_SCALE
            o_ref[:, hh * HEAD_DIM:(hh + 1) * HEAD_DIM] = r.astype(_BF16)

    @pl.when(j == 2)
    def _():
        o_ref[:, :NA_WIDTH] = (acc[:, :NA_WIDTH] * NA_SCALE).astype(_BF16)
        o_ref[:, NA_WIDTH:] = acc[:, NA_WIDTH:].astype(_BF16)

    @pl.when(j == 4)
    def _():
        o_ref[:, :2 * MLP_WIDTH] = _gelu_tanh(acc[:, :2 * MLP_WIDTH]).astype(_BF16)
        o_ref[:, 2 * MLP_WIDTH:] = acc[:, 2 * MLP_WIDTH:].astype(_BF16)

    @pl.when((j == 1) | (j == 3))
    def _():
        o_ref[...] = acc.astype(_BF16)


def _in_proj(x2d, pre_g, scale, shift, w_bf16, cos, sa, sb, *, tm, seq):
    T, D = x2d.shape
    nb = scale.shape[0]
    tiles_per_mod = (T // nb) // tm
    tiles_per_seq = seq // tm
    return pl.pallas_call(
        _in_kernel,
        grid=(T // tm, IN_WIDTH // IN_TILE_N),
        in_specs=[
            pl.BlockSpec((tm, D), lambda i, j: (i, 0)),
            pl.BlockSpec((1, D), lambda i, j: (0, 0)),
            pl.BlockSpec((None, 1, D), lambda i, j: (i // tiles_per_mod, 0, 0)),
            pl.BlockSpec((None, 1, D), lambda i, j: (i // tiles_per_mod, 0, 0)),
            pl.BlockSpec((D, IN_TILE_N), lambda i, j: (0, j)),
            pl.BlockSpec((tm, HEAD_DIM), lambda i, j: (i % tiles_per_seq, 0)),
            pl.BlockSpec((tm, HEAD_DIM), lambda i, j: (i % tiles_per_seq, 0)),
            pl.BlockSpec((tm, HEAD_DIM), lambda i, j: (i % tiles_per_seq, 0)),
        ],
        out_specs=pl.BlockSpec((tm, IN_TILE_N), lambda i, j: (i, j)),
        out_shape=jax.ShapeDtypeStruct((T, IN_WIDTH), _BF16),
        scratch_shapes=[pltpu.VMEM((tm, D), _BF16)],
        name="in_proj",
        compiler_params=_params("parallel", "arbitrary"),
    )(x2d, pre_g.reshape(1, D), scale, shift, w_bf16, cos, sa, sb)


def _softmax_pv(qq, sources):
    scores = []
    for k, _, bias in sources:
        s = _dot_nt(qq, k)
        if bias is not None:
            s = s + bias
        scores.append(s)
    m = jnp.max(scores[0], axis=-1, keepdims=True)
    for s in scores[1:]:
        m = jnp.maximum(m, jnp.max(s, axis=-1, keepdims=True))
    l = None
    o = None
    for s, (_, v, _) in zip(scores, sources):
        p = jnp.exp(s - m)
        ps = jnp.sum(p, axis=-1, keepdims=True)
        pv = jnp.dot(p.astype(_BF16), v, preferred_element_type=_F32)
        l = ps if l is None else l + ps
        o = pv if o is None else o + pv
    return o / l


def _branch_out(o, g, factor, z_ref, o_ref):
    z = z_ref[...].astype(_F32)
    o_ref[...] = (_rms(o, g) * factor * _silu(z)).astype(_BF16)


def _da_kernel(*refs, n_src, lam_init):
    q_ref, z_ref = refs[0], refs[1]
    kv_refs = refs[2:2 + 2 * n_src]
    lam_ref, g_ref, o_ref = refs[2 + 2 * n_src:]

    q = q_ref[...]
    tq = q.shape[0]
    lane = lax.broadcasted_iota(jnp.int32, q.shape, 1)
    zero = jnp.zeros_like(q)
    qq = jnp.concatenate([jnp.where(lane < DA_QK_DIM, q, zero),
                          jnp.where(lane >= DA_QK_DIM, q, zero)], axis=0)
    sources = [(kv_refs[2 * i][...], kv_refs[2 * i + 1][...], None) for i in range(n_src)]
    o = _softmax_pv(qq, sources)

    lp = lam_ref[...]
    lam = (jnp.exp(jnp.sum(lp[0:1] * lp[1:2], axis=-1, keepdims=True))
           - jnp.exp(jnp.sum(lp[2:3] * lp[3:4], axis=-1, keepdims=True)) + lam_init)
    oa = o[:tq] - lam * o[tq:]
    _branch_out(oa, g_ref[...], 1.0 - lam_init, z_ref, o_ref)


def _diff_attn(q_proj, kv_projs, lam_params, da_g, *, tq, lam_init):
    B, Sq, _ = q_proj.shape
    in_specs = [
        pl.BlockSpec((None, tq, HEAD_DIM), lambda b, h, i: (b, i, COL_AQ + h)),
        pl.BlockSpec((None, tq, HEAD_DIM), lambda b, h, i: (b, i, COL_AZ + h)),
    ]
    args = [q_proj, q_proj]
    for kp in kv_projs:
        sk = kp.shape[1]
        in_specs.append(pl.BlockSpec((None, sk, HEAD_DIM), lambda b, h, i: (b, 0, COL_AK + h)))
        in_specs.append(pl.BlockSpec((None, sk, HEAD_DIM), lambda b, h, i: (b, 0, COL_AV + h)))
        args += [kp, kp]
    in_specs += [
        pl.BlockSpec((4, DA_QK_DIM), lambda b, h, i: (0, 0)),
        pl.BlockSpec((1, HEAD_DIM), lambda b, h, i: (0, 0)),
    ]
    args += [lam_params, da_g.reshape(1, HEAD_DIM)]
    return pl.pallas_call(
        functools.partial(_da_kernel, n_src=len(kv_projs), lam_init=lam_init),
        grid=(B, DA_HEADS, Sq // tq),
        in_specs=in_specs,
        out_specs=pl.BlockSpec((None, tq, HEAD_DIM), lambda b, h, i: (b, i, h)),
        out_shape=jax.ShapeDtypeStruct((B, Sq, DA_WIDTH), _BF16),
        name="diff_attn",
        compiler_params=_params("parallel", "parallel", "parallel"),
    )(*args)


def _na_kernel(q_ref, z_ref, k_ref, v_ref, kc_ref, vc_ref, bias_ref, g_ref, o_ref):
    rb = pl.program_id(2)
    rows = k_ref.shape[0] // GRID_W
    kr0 = jnp.clip(rb * NA_BLOCK_ROWS - NA_WIN_H // 2, 0, rows - NA_KEY_ROWS)
    start = pl.multiple_of(kr0 * GRID_W, GRID_W)
    k_win = k_ref[pl.ds(start, NA_K), :]
    v_win = v_ref[pl.ds(start, NA_K), :]
    o = _softmax_pv(q_ref[...], [(k_win, v_win, bias_ref[...]), (kc_ref[...], vc_ref[...], None)])
    _branch_out(o, g_ref[...], 1.0, z_ref, o_ref)


def _na_bias_kernel(r_ref, o_ref, *, rows):
    shape = (GRID_W, 2 * GRID_W)
    c = lax.broadcasted_iota(jnp.int32, shape, 0)
    lane = lax.broadcasted_iota(jnp.int32, shape, 1)
    kc = lane & (GRID_W - 1)
    cs = jnp.clip(c - NA_WIN_W // 2, 0, GRID_W - NA_WIN_W)
    col_valid = (kc >= cs) & (kc < cs + NA_WIN_W)
    first_key_row = lane < GRID_W
    toeplitz = []
    for t in range(2 * NA_WIN_H):
        row = jnp.broadcast_to(r_ref[t:t + 1, :], shape)
        x = pltpu.roll(row, 2 * GRID_W - (NA_WIN_W - 1), 1, stride=1, stride_axis=0)
        toeplitz.append(jnp.where(col_valid, x, MASK_VALUE))
    masked = jnp.full(shape, MASK_VALUE, _F32)
    for cfg, (r0, kr0) in enumerate(_na_block_configs(rows)):
        for ri in range(NA_BLOCK_ROWS):
            r = r0 + ri
            rs = min(max(r - NA_WIN_H // 2, 0), rows - NA_WIN_H)
            for j in range(NA_KEY_ROWS // 2):
                kra = kr0 + 2 * j
                valid_a = rs <= kra < rs + NA_WIN_H
                valid_b = rs <= kra + 1 < rs + NA_WIN_H
                t = kra - r + NA_WIN_H
                if valid_a and valid_b:
                    blk = toeplitz[t]
                elif valid_a:
                    blk = jnp.where(first_key_row, toeplitz[t], MASK_VALUE)
                elif valid_b:
                    blk = jnp.where(first_key_row, MASK_VALUE, toeplitz[t])
                else:
                    blk = masked
                o_ref[cfg, ri * GRID_W:(ri + 1) * GRID_W, j * 2 * GRID_W:(j + 1) * 2 * GRID_W] = blk


def _na_block_configs(rows):
    return ((0, 0), (2 * NA_BLOCK_ROWS, 2 * NA_BLOCK_ROWS - NA_WIN_H // 2),
            (rows - NA_BLOCK_ROWS, rows - NA_KEY_ROWS))


def _na_bias_tables(rpb, rows):
    H, nr, ncol = rpb.shape
    lo = jnp.pad(rpb, ((0, 0), (1, 0), (0, 2 * GRID_W - ncol)))
    hi = jnp.pad(rpb, ((0, 0), (0, 1), (GRID_W, GRID_W - ncol)))
    return pl.pallas_call(
        functools.partial(_na_bias_kernel, rows=rows),
        grid=(H,),
        in_specs=[pl.BlockSpec((None, nr + 1, 2 * GRID_W), lambda h: (h, 0, 0))],
        out_specs=pl.BlockSpec((3, None, NA_Q, NA_K), lambda h: (0, h, 0, 0)),
        out_shape=jax.ShapeDtypeStruct((3, H, NA_Q, NA_K), _F32),
        name="na_bias",
        compiler_params=_params("parallel"),
    )(lo + hi)


def _nbr_attn(proj, proj_ctx, bias_tabs, na_g):
    B, S, _ = proj.shape
    C = proj_ctx.shape[1]
    n_blocks = S // NA_Q

    def bias_idx(b, h, i):
        return (jnp.where(i == 0, 0, jnp.where(i == n_blocks - 1, 2, 1)), h, 0, 0)

    return pl.pallas_call(
        _na_kernel,
        grid=(B, NA_HEADS, n_blocks),
        in_specs=[
            pl.BlockSpec((None, NA_Q, HEAD_DIM), lambda b, h, i: (b, i, COL_BQ + h)),
            pl.BlockSpec((None, NA_Q, HEAD_DIM), lambda b, h, i: (b, i, COL_BZ + h)),
            pl.BlockSpec((None, S, HEAD_DIM), lambda b, h, i: (b, 0, COL_BK + h)),
            pl.BlockSpec((None, S, HEAD_DIM), lambda b, h, i: (b, 0, COL_BV + h)),
            pl.BlockSpec((None, C, HEAD_DIM), lambda b, h, i: (b, 0, COL_BK + h)),
            pl.BlockSpec((None, C, HEAD_DIM), lambda b, h, i: (b, 0, COL_BV + h)),
            pl.BlockSpec((None, None, NA_Q, NA_K), bias_idx),
            pl.BlockSpec((None, 1, HEAD_DIM), lambda b, h, i: (h, 0, 0)),
        ],
        out_specs=pl.BlockSpec((None, NA_Q, HEAD_DIM), lambda b, h, i: (b, i, h)),
        out_shape=jax.ShapeDtypeStruct((B, S, NA_WIDTH), _BF16),
        name="nbr_attn",
        compiler_params=_params("parallel", "parallel", "parallel"),
    )(proj, proj, proj, proj, proj_ctx, proj_ctx, bias_tabs, na_g.reshape(NA_HEADS, 1, HEAD_DIM))


def _ctx_attn_kernel(q_ref, z_ref, k_ref, v_ref, g_ref, o_ref):
    o = _softmax_pv(q_ref[...], [(k_ref[...], v_ref[...], None)])
    _branch_out(o, g_ref[...], 1.0, z_ref, o_ref)


def _ctx_attn(proj_ctx, na_g):
    B, C, _ = proj_ctx.shape
    return pl.pallas_call(
        _ctx_attn_kernel,
        grid=(B, NA_HEADS),
        in_specs=[
            pl.BlockSpec((None, C, HEAD_DIM), lambda b, h: (b, 0, COL_BQ + h)),
            pl.BlockSpec((None, C, HEAD_DIM), lambda b, h: (b, 0, COL_BZ + h)),
            pl.BlockSpec((None, C, HEAD_DIM), lambda b, h: (b, 0, COL_BK + h)),
            pl.BlockSpec((None, C, HEAD_DIM), lambda b, h: (b, 0, COL_BV + h)),
            pl.BlockSpec((None, 1, HEAD_DIM), lambda b, h: (h, 0, 0)),
        ],
        out_specs=pl.BlockSpec((None, C, HEAD_DIM), lambda b, h: (b, 0, h)),
        out_shape=jax.ShapeDtypeStruct((B, C, NA_WIDTH), _BF16),
        name="ctx_attn",
        compiler_params=_params("parallel", "parallel"),
    )(proj_ctx, proj_ctx, proj_ctx, proj_ctx, na_g.reshape(NA_HEADS, 1, HEAD_DIM))


def _mlp_kernel(u_ref, v_ref, z_ref, lng_ref, lnb_ref, ws_ref, bs_ref, g_ref, o_ref):
    n_chunks = u_ref.shape[0] // CHUNK
    for g in range(MLP_GROUPS):
        cols = slice(g * HEAD_DIM, (g + 1) * HEAD_DIM)
        w = ws_ref[g].astype(_BF16)
        bs = bs_ref[g]
        ln_g, ln_b, out_g = lng_ref[:, cols], lnb_ref[:, cols], g_ref[:, cols]
        for c in range(n_chunks):
            rows = slice(c * CHUNK, (c + 1) * CHUNK)
            v = v_ref[rows, cols].astype(_F32)
            vc = v - jnp.mean(v, axis=-1, keepdims=True)
            vn = vc * lax.rsqrt(jnp.mean(vc * vc, axis=-1, keepdims=True) + EPS) * ln_g + ln_b
            mixed = jnp.dot(w, vn.astype(_BF16), preferred_element_type=_F32) + bs
            om = u_ref[rows, cols].astype(_F32) * mixed
            z = z_ref[rows, cols].astype(_F32)
            o_ref[rows, cols] = (_rms(om, out_g) * _silu(z)).astype(_BF16)


def _gmlp(proj, ln_g, ln_b, ws, bs, out_g, *, rows_per_step):
    B, S, _ = proj.shape
    return pl.pallas_call(
        _mlp_kernel,
        grid=(B, S // rows_per_step),
        in_specs=[
            pl.BlockSpec((None, rows_per_step, MLP_WIDTH), lambda b, i: (b, i, COL_CU)),
            pl.BlockSpec((None, rows_per_step, MLP_WIDTH), lambda b, i: (b, i, COL_CV)),
            pl.BlockSpec((None, rows_per_step, MLP_WIDTH), lambda b, i: (b, i, COL_CZ)),
            pl.BlockSpec((1, MLP_WIDTH), lambda b, i: (0, 0)),
            pl.BlockSpec((1, MLP_WIDTH), lambda b, i: (0, 0)),
            pl.BlockSpec((MLP_GROUPS, CHUNK, CHUNK), lambda b, i: (0, 0, 0)),
            pl.BlockSpec((MLP_GROUPS, CHUNK, 1), lambda b, i: (0, 0, 0)),
            pl.BlockSpec((1, MLP_WIDTH), lambda b, i: (0, 0)),
        ],
        out_specs=pl.BlockSpec((None, rows_per_step, MLP_WIDTH), lambda b, i: (b, i, 0)),
        out_shape=jax.ShapeDtypeStruct((B, S, MLP_WIDTH), _BF16),
        name="gmlp",
        compiler_params=_params("parallel", "parallel"),
    )(proj, proj, proj, ln_g.reshape(1, MLP_WIDTH), ln_b.reshape(1, MLP_WIDTH), ws,
      bs.reshape(MLP_GROUPS, CHUNK, 1), out_g.reshape(1, MLP_WIDTH))


def _out_kernel(ya_ref, yn_ref, ym_ref, w_ref, x_ref, gate_ref, g_ref, o_ref):
    y = jnp.dot(ya_ref[...], w_ref[:DA_WIDTH, :], preferred_element_type=_F32)
    y = y + jnp.dot(yn_ref[...], w_ref[DA_WIDTH:DA_WIDTH + NA_WIDTH, :], preferred_element_type=_F32)
    y = y + jnp.dot(ym_ref[...], w_ref[DA_WIDTH + NA_WIDTH:, :], preferred_element_type=_F32)
    o_ref[...] = x_ref[...] + gate_ref[...] * _rms(y, g_ref[...])


def _out_proj(ya, yn, ym, w_bf16, x2d, gate, post_g, *, tm):
    T, D = x2d.shape
    nb = gate.shape[0]
    tiles_per_mod = (T // nb) // tm
    return pl.pallas_call(
        _out_kernel,
        grid=(T // tm,),
        in_specs=[
            pl.BlockSpec((tm, DA_WIDTH), lambda i: (i, 0)),
            pl.BlockSpec((tm, NA_WIDTH), lambda i: (i, 0)),
            pl.BlockSpec((tm, MLP_WIDTH), lambda i: (i, 0)),
            pl.BlockSpec((MIX_WIDTH, D), lambda i: (0, 0)),
            pl.BlockSpec((tm, D), lambda i: (i, 0)),
            pl.BlockSpec((None, 1, D), lambda i: (i // tiles_per_mod, 0, 0)),
            pl.BlockSpec((1, D), lambda i: (0, 0)),
        ],
        out_specs=pl.BlockSpec((tm, D), lambda i: (i, 0)),
        out_shape=jax.ShapeDtypeStruct((T, D), _F32),
        name="out_proj",
        compiler_params=_params("parallel"),
    )(ya.reshape(T, DA_WIDTH), yn.reshape(T, NA_WIDTH), ym.reshape(T, MLP_WIDTH), w_bf16, x2d,
      gate, post_g.reshape(1, D))


def _rope_tables(seq):
    n = DA_QK_DIM // 4
    freqs = ROPE_BASE ** (-jnp.arange(n, dtype=_F32) / n)
    pos = jnp.arange(seq)
    row = (pos // GRID_W).astype(_F32)[:, None] * freqs
    col = (pos % GRID_W).astype(_F32)[:, None] * freqs
    zeros = jnp.zeros((seq, n), _F32)

    def one_map(fn_first, fn_second):
        return jnp.concatenate([fn_first(row), fn_second(row), fn_first(col), fn_second(col)], axis=-1)

    cos = one_map(jnp.cos, jnp.cos)
    sa = one_map(lambda a: -jnp.sin(a), lambda a: zeros)
    sb = one_map(lambda a: zeros, jnp.sin)
    return tuple(jnp.concatenate([t, t], axis=-1) for t in (cos, sa, sb))


def _identity_rope_tables(seq):
    return (jnp.ones((seq, HEAD_DIM), _F32), jnp.zeros((seq, HEAD_DIM), _F32),
            jnp.zeros((seq, HEAD_DIM), _F32))


def kernel(x, c, ctx, c_ctx, ada_w, ada_b, pre_g, post_g, w_in, w_out, lam_q1, lam_k1, lam_q2, lam_k2,
           da_g, na_rpb, na_g, mlp_ln_g, mlp_ln_b, mlp_ws, mlp_bs, mlp_g):
    B, S, D = x.shape
    C = ctx.shape[1]
    L = ada_w.shape[0]
    assert S % NA_Q == 0 and S // GRID_W >= NA_KEY_ROWS and C % CHUNK == 0

    cc = jnp.concatenate([c, c_ctx[None], jnp.zeros((8 - B - 1, D), _F32)], axis=0)
    mod = _ada(cc, ada_w, ada_b)

    rope = _rope_tables(S)
    rope_id = _identity_rope_tables(C)
    x2d = x.reshape(B * S, D)
    xc2d = ctx.reshape(B * C, D)

    for l in range(L):
        lam_init = 0.8 - 0.6 * math.exp(-0.3 * l)
        ctx_out = l < L - 1
        shift, scale, gate = (mod[l, :B, i * D:(i + 1) * D].reshape(B, 1, D) for i in range(3))
        shift_c, scale_c, gate_c = (mod[l, B:B + 1, i * D:(i + 1) * D].reshape(1, 1, D) for i in range(3))
        w_in_b = w_in[l].astype(_BF16)
        w_out_b = w_out[l].astype(_BF16)
        lam_params = jnp.stack([lam_q1[l], lam_k1[l], lam_q2[l], lam_k2[l]])

        proj = _in_proj(x2d, pre_g[l], scale, shift, w_in_b, *rope, tm=512, seq=S).reshape(B, S, IN_WIDTH)
        proj_c = _in_proj(xc2d, pre_g[l], scale_c, shift_c, w_in_b, *rope_id, tm=C, seq=C).reshape(B, C, IN_WIDTH)

        ya = _diff_attn(proj, [proj, proj_c], lam_params, da_g[l], tq=128, lam_init=lam_init)
        yn = _nbr_attn(proj, proj_c, _na_bias_tables(na_rpb[l], S // GRID_W), na_g[l])
        ym = _gmlp(proj, mlp_ln_g[l], mlp_ln_b[l], mlp_ws[l], mlp_bs[l], mlp_g[l], rows_per_step=512)
        x2d_new = _out_proj(ya, yn, ym, w_out_b, x2d, gate, post_g[l], tm=256)

        if ctx_out:
            yca = _diff_attn(proj_c, [proj_c], lam_params, da_g[l], tq=C, lam_init=lam_init)
            ycn = _ctx_attn(proj_c, na_g[l])
            ycm = _gmlp(proj_c, mlp_ln_g[l], mlp_ln_b[l], mlp_ws[l], mlp_bs[l], mlp_g[l], rows_per_step=C)
            xc2d = _out_proj(yca, ycn, ycm, w_out_b, xc2d, gate_c, post_g[l], tm=C)
        x2d = x2d_new

    return x2d.reshape(B, S, D)
```

```python
import functools
import math

import numpy as np
import jax
import jax.numpy as jnp
from jax import lax
from jax.experimental import pallas as pl
from jax.experimental.pallas import tpu as pltpu

D_MODEL = 2048
DEPTH = 2
GRID_W = 64
HEAD_DIM = 128
DA_HEADS = 6
NA_HEADS = 6
MLP_GROUPS = 4
DA_WIDTH = DA_HEADS * HEAD_DIM
NA_WIDTH = NA_HEADS * HEAD_DIM
MLP_WIDTH = MLP_GROUPS * HEAD_DIM
DA_QK_DIM = HEAD_DIM // 2
NA_WIN_H = 8
NA_WIN_W = 16
CHUNK = 128
ROPE_BASE = 10000.0
EPS = 1e-6
IN_WIDTH = 4 * DA_WIDTH + 4 * NA_WIDTH + 3 * MLP_WIDTH
MIX_WIDTH = DA_WIDTH + NA_WIDTH + MLP_WIDTH

COL_AQ, COL_AK, COL_AV, COL_AZ = 0, 6, 12, 18
COL_BQ, COL_BK, COL_BV, COL_BZ = 24, 30, 36, 42
COL_CU, COL_CV, COL_CZ = 12, 13, 14

IN_TILE_N = 2 * DA_WIDTH
DA_SCALE = DA_QK_DIM ** -0.5 * math.log2(math.e)
NA_SCALE = HEAD_DIM ** -0.5
DA_KEY_CHUNK = 512
DA_Q_TILE = 128
DA_TILES_PER_STEP = 8

NA_BLOCK_ROWS = 4
NA_KEY_ROWS = NA_BLOCK_ROWS + NA_WIN_H
NA_Q = NA_BLOCK_ROWS * GRID_W
NA_K = NA_KEY_ROWS * GRID_W
MASK_VALUE = -1e30

VMEM_LIMIT_BYTES = 56 * 1024 * 1024

_BF16 = jnp.bfloat16
_F32 = jnp.float32


def _silu(x):
    return x * (1.0 / (1.0 + jnp.exp(-x)))


def _gelu_tanh(x):
    c = math.sqrt(2.0 / math.pi)
    return 0.5 * x * (1.0 + jnp.tanh(c * (x + 0.044715 * (x * x * x))))


def _rms(x, g):
    return x * lax.rsqrt(jnp.mean(x * x, axis=-1, keepdims=True) + EPS) * g


def _dot_nt(a, b):
    return lax.dot_general(a, b, (((1,), (1,)), ((), ())), preferred_element_type=_F32)


def _params(*sem):
    return pltpu.CompilerParams(dimension_semantics=sem, vmem_limit_bytes=VMEM_LIMIT_BYTES)


def _ada_kernel(c_ref, w_ref, b_ref, o_ref):
    sc = _silu(c_ref[...]).astype(_BF16)
    o_ref[...] = jnp.dot(sc, w_ref[...].astype(_BF16), preferred_element_type=_F32) + b_ref[...]


def _ada(cc, ada_w, ada_b):
    L, D, N = ada_w.shape
    R = cc.shape[0]
    tn = 768
    return pl.pallas_call(
        _ada_kernel,
        grid=(L, N // tn),
        in_specs=[
            pl.BlockSpec((R, D), lambda l, j: (0, 0)),
            pl.BlockSpec((None, D, tn), lambda l, j: (l, 0, j)),
            pl.BlockSpec((None, 1, tn), lambda l, j: (l, 0, j)),
        ],
        out_specs=pl.BlockSpec((None, R, tn), lambda l, j: (l, 0, j)),
        out_shape=jax.ShapeDtypeStruct((L, R, N), _F32),
        name="ada_mod",
        compiler_params=_params("parallel", "parallel"),
    )(cc, ada_w, ada_b.reshape(L, 1, N))


def _in_kernel(x_ref, g_ref, scale_ref, shift_ref, w_ref, cos_ref, sa_ref, sb_ref, o_ref, h_ref):
    j = pl.program_id(1)

    @pl.when(j == 0)
    def _():
        x = x_ref[...]
        h = _rms(x, g_ref[...]) * (1.0 + scale_ref[...]) + shift_ref[...]
        h_ref[...] = h.astype(_BF16)

    acc = jnp.dot(h_ref[...], w_ref[...], preferred_element_type=_F32)

    @pl.when(j == 0)
    def _():
        cos, sa, sb = cos_ref[...], sa_ref[...], sb_ref[...]
        for hh in range(2 * DA_HEADS):
            xs = acc[:, hh * HEAD_DIM:(hh + 1) * HEAD_DIM]
            r = xs * cos + pltpu.roll(xs, HEAD_DIM - 16, 1) * sa + pltpu.roll(xs, 16, 1) * sb
            if hh < DA_HEADS:
                r = r * DA_SCALE
            o_ref[:, hh * HEAD_DIM:(hh + 1) * HEAD_DIM] = r.astype(_BF16)

    @pl.when(j == 2)
    def _():
        o_ref[:, :NA_WIDTH] = (acc[:, :NA_WIDTH] * NA_SCALE).astype(_BF16)
        o_ref[:, NA_WIDTH:] = acc[:, NA_WIDTH:].astype(_BF16)

    @pl.when(j == 4)
    def _():
        o_ref[:, :2 * MLP_WIDTH] = _gelu_tanh(acc[:, :2 * MLP_WIDTH]).astype(_BF16)
        o_ref[:, 2 * MLP_WIDTH:] = acc[:, 2 * MLP_WIDTH:].astype(_BF16)

    @pl.when((j == 1) | (j == 3))
    def _():
        o_ref[...] = acc.astype(_BF16)


def _in_proj(x2d, pre_g, scale, shift, w_bf16, cos, sa, sb, *, tm, seq):
    T, D = x2d.shape
    nb = scale.shape[0]
    tiles_per_mod = (T // nb) // tm
    tiles_per_seq = seq // tm
    return pl.pallas_call(
        _in_kernel,
        grid=(T // tm, IN_WIDTH // IN_TILE_N),
        in_specs=[
            pl.BlockSpec((tm, D), lambda i, j: (i, 0)),
            pl.BlockSpec((1, D), lambda i, j: (0, 0)),
            pl.BlockSpec((None, 1, D), lambda i, j: (i // tiles_per_mod, 0, 0)),
            pl.BlockSpec((None, 1, D), lambda i, j: (i // tiles_per_mod, 0, 0)),
            pl.BlockSpec((D, IN_TILE_N), lambda i, j: (0, j)),
            pl.BlockSpec((tm, HEAD_DIM), lambda i, j: (i % tiles_per_seq, 0)),
            pl.BlockSpec((tm, HEAD_DIM), lambda i, j: (i % tiles_per_seq, 0)),
            pl.BlockSpec((tm, HEAD_DIM), lambda i, j: (i % tiles_per_seq, 0)),
        ],
        out_specs=pl.BlockSpec((tm, IN_TILE_N), lambda i, j: (i, j)),
        out_shape=jax.ShapeDtypeStruct((T, IN_WIDTH), _BF16),
        scratch_shapes=[pltpu.VMEM((tm, D), _BF16)],
        name="in_proj",
        compiler_params=_params("parallel", "arbitrary"),
    )(x2d, pre_g.reshape(1, D), scale, shift, w_bf16, cos, sa, sb)


def _softmax_pv(qq, sources):
    scores = []
    for k, _, bias in sources:
        s = _dot_nt(qq, k)
        if bias is not None:
            s = s + bias
        scores.append(s)
    m = jnp.max(scores[0], axis=-1, keepdims=True)
    for s in scores[1:]:
        m = jnp.maximum(m, jnp.max(s, axis=-1, keepdims=True))
    l = None
    o = None
    for s, (_, v, _) in zip(scores, sources):
        p = jnp.exp(s - m)
        ps = jnp.sum(p, axis=-1, keepdims=True)
        pv = jnp.dot(p.astype(_BF16), v, preferred_element_type=_F32)
        l = ps if l is None else l + ps
        o = pv if o is None else o + pv
    return o / l


def _branch_out(o, g, factor, z_ref, o_ref):
    z = z_ref[...].astype(_F32)
    o_ref[...] = (_rms(o, g) * factor * _silu(z)).astype(_BF16)


def _da_kernel(*refs, n_src, lam_init):
    q_ref, z_ref = refs[0], refs[1]
    kv_refs = refs[2:2 + 2 * n_src]
    lam_ref, g_ref, o_ref, s_ref = refs[2 + 2 * n_src:]
    tq = DA_Q_TILE
    n_tiles = q_ref.shape[0] // tq

    chunks = []
    col = 0
    for i in range(n_src):
        k_ref, v_ref = kv_refs[2 * i], kv_refs[2 * i + 1]
        size = k_ref.shape[0]
        ck = min(size, DA_KEY_CHUNK)
        for c in range(size // ck):
            chunks.append((k_ref, v_ref, c * ck, ck, col))
            col += ck

    lp = lam_ref[...]
    lam = (jnp.exp(jnp.sum(lp[0:1] * lp[1:2], axis=-1, keepdims=True))
           - jnp.exp(jnp.sum(lp[2:3] * lp[3:4], axis=-1, keepdims=True)) + lam_init)

    def scores_and_max(t):
        q = q_ref[t * tq:(t + 1) * tq, :]
        lane = lax.broadcasted_iota(jnp.int32, q.shape, 1)
        zero = jnp.zeros_like(q)
        qq = jnp.concatenate([jnp.where(lane < DA_QK_DIM, q, zero),
                              jnp.where(lane >= DA_QK_DIM, q, zero)], axis=0)
        mx = None
        for k_ref, _, r0, ck, c0 in chunks:
            s = _dot_nt(qq, k_ref[r0:r0 + ck, :])
            s_ref[t % 2, :, c0:c0 + ck] = s
            for jj in range(ck // HEAD_DIM):
                part = s[:, jj * HEAD_DIM:(jj + 1) * HEAD_DIM]
                mx = part if mx is None else jnp.maximum(mx, part)
        return jnp.max(mx, axis=-1, keepdims=True)

    def softmax_pv_out(t, m):
        acc = None
        for _, v_ref, r0, ck, c0 in chunks:
            p = jnp.exp2(s_ref[t % 2, :, c0:c0 + ck] - m).astype(_BF16)
            v_aug = jnp.concatenate([v_ref[r0:r0 + ck, :], jnp.ones((ck, HEAD_DIM), _BF16)], axis=1)
            pv = jnp.dot(p, v_aug, preferred_element_type=_F32)
            acc = pv if acc is None else acc + pv
        o = acc[:, :HEAD_DIM] / acc[:, HEAD_DIM:]
        oa = o[:tq] - lam * o[tq:]
        rows = slice(t * tq, (t + 1) * tq)
        z = z_ref[rows, :].astype(_F32)
        o_ref[rows, :] = (_rms(oa, g_ref[...]) * (1.0 - lam_init) * _silu(z)).astype(_BF16)

    row_max = scores_and_max(0)
    for t in range(1, n_tiles):
        next_max = scores_and_max(t)
        softmax_pv_out(t - 1, row_max)
        row_max = next_max
    softmax_pv_out(n_tiles - 1, row_max)


def _diff_attn(q_proj, kv_projs, lam_params, da_g, *, lam_init):
    B, Sq, _ = q_proj.shape
    tq = min(DA_TILES_PER_STEP * DA_Q_TILE, Sq)
    in_specs = [
        pl.BlockSpec((None, tq, HEAD_DIM), lambda b, h, i: (b, i, COL_AQ + h)),
        pl.BlockSpec((None, tq, HEAD_DIM), lambda b, h, i: (b, i, COL_AZ + h)),
    ]
    args = [q_proj, q_proj]
    for kp in kv_projs:
        sk = kp.shape[1]
        in_specs.append(pl.BlockSpec((None, sk, HEAD_DIM), lambda b, h, i: (b, 0, COL_AK + h)))
        in_specs.append(pl.BlockSpec((None, sk, HEAD_DIM), lambda b, h, i: (b, 0, COL_AV + h)))
        args += [kp, kp]
    in_specs += [
        pl.BlockSpec((4, DA_QK_DIM), lambda b, h, i: (0, 0)),
        pl.BlockSpec((1, HEAD_DIM), lambda b, h, i: (0, 0)),
    ]
    args += [lam_params, da_g.reshape(1, HEAD_DIM)]
    return pl.pallas_call(
        functools.partial(_da_kernel, n_src=len(kv_projs), lam_init=lam_init),
        grid=(B, DA_HEADS, Sq // tq),
        in_specs=in_specs,
        out_specs=pl.BlockSpec((None, tq, HEAD_DIM), lambda b, h, i: (b, i, h)),
        out_shape=jax.ShapeDtypeStruct((B, Sq, DA_WIDTH), _BF16),
        scratch_shapes=[pltpu.VMEM((2, 2 * DA_Q_TILE, sum(kp.shape[1] for kp in kv_projs)), _F32)],
        name="diff_attn",
        compiler_params=_params("parallel", "parallel", "parallel"),
    )(*args)


def _na_kernel(q_ref, z_ref, k_ref, v_ref, kc_ref, vc_ref, bias_ref, g_ref, o_ref):
    rb = pl.program_id(1)
    rows = k_ref.shape[0] // GRID_W
    kr0 = jnp.clip(rb * NA_BLOCK_ROWS - NA_WIN_H // 2, 0, rows - NA_KEY_ROWS)
    start = pl.multiple_of(kr0 * GRID_W, GRID_W)
    for h in range(NA_HEADS):
        cols = slice(h * HEAD_DIM, (h + 1) * HEAD_DIM)
        k_win = k_ref[pl.ds(start, NA_K), cols]
        v_win = v_ref[pl.ds(start, NA_K), cols]
        o = _softmax_pv(q_ref[:, cols], [(k_win, v_win, bias_ref[h]), (kc_ref[:, cols], vc_ref[:, cols], None)])
        z = z_ref[:, cols].astype(_F32)
        o_ref[:, cols] = (_rms(o, g_ref[:, cols]) * _silu(z)).astype(_BF16)


def _na_bias_kernel(r_ref, o_ref, *, rows):
    shape = (GRID_W, 2 * GRID_W)
    c = lax.broadcasted_iota(jnp.int32, shape, 0)
    lane = lax.broadcasted_iota(jnp.int32, shape, 1)
    kc = lane & (GRID_W - 1)
    cs = jnp.clip(c - NA_WIN_W // 2, 0, GRID_W - NA_WIN_W)
    col_valid = (kc >= cs) & (kc < cs + NA_WIN_W)
    first_key_row = lane < GRID_W
    toeplitz = []
    for t in range(2 * NA_WIN_H):
        row = jnp.broadcast_to(r_ref[t:t + 1, :], shape)
        x = pltpu.roll(row, 2 * GRID_W - (NA_WIN_W - 1), 1, stride=1, stride_axis=0)
        toeplitz.append(jnp.where(col_valid, x, MASK_VALUE))
    masked = jnp.full(shape, MASK_VALUE, _F32)
    for cfg, (r0, kr0) in enumerate(_na_block_configs(rows)):
        for ri in range(NA_BLOCK_ROWS):
            r = r0 + ri
            rs = min(max(r - NA_WIN_H // 2, 0), rows - NA_WIN_H)
            for j in range(NA_KEY_ROWS // 2):
                kra = kr0 + 2 * j
                valid_a = rs <= kra < rs + NA_WIN_H
                valid_b = rs <= kra + 1 < rs + NA_WIN_H
                t = kra - r + NA_WIN_H
                if valid_a and valid_b:
                    blk = toeplitz[t]
                elif valid_a:
                    blk = jnp.where(first_key_row, toeplitz[t], MASK_VALUE)
                elif valid_b:
                    blk = jnp.where(first_key_row, MASK_VALUE, toeplitz[t])
                else:
                    blk = masked
                o_ref[cfg, ri * GRID_W:(ri + 1) * GRID_W, j * 2 * GRID_W:(j + 1) * 2 * GRID_W] = blk


def _na_block_configs(rows):
    return ((0, 0), (2 * NA_BLOCK_ROWS, 2 * NA_BLOCK_ROWS - NA_WIN_H // 2),
            (rows - NA_BLOCK_ROWS, rows - NA_KEY_ROWS))


def _na_bias_tables(rpb, rows):
    H, nr, ncol = rpb.shape
    lo = jnp.pad(rpb, ((0, 0), (1, 0), (0, 2 * GRID_W - ncol)))
    hi = jnp.pad(rpb, ((0, 0), (0, 1), (GRID_W, GRID_W - ncol)))
    return pl.pallas_call(
        functools.partial(_na_bias_kernel, rows=rows),
        grid=(H,),
        in_specs=[pl.BlockSpec((None, nr + 1, 2 * GRID_W), lambda h: (h, 0, 0))],
        out_specs=pl.BlockSpec((3, None, NA_Q, NA_K), lambda h: (0, h, 0, 0)),
        out_shape=jax.ShapeDtypeStruct((3, H, NA_Q, NA_K), _F32),
        name="na_bias",
        compiler_params=_params("parallel"),
    )(lo + hi)


def _nbr_attn(proj, proj_ctx, bias_tabs, na_g):
    B, S, _ = proj.shape
    C = proj_ctx.shape[1]
    n_blocks = S // NA_Q

    def bias_idx(b, i):
        return (jnp.where(i == 0, 0, jnp.where(i == n_blocks - 1, 2, 1)), 0, 0, 0)

    cq, ck, cv, cz = (c * HEAD_DIM // NA_WIDTH for c in (COL_BQ, COL_BK, COL_BV, COL_BZ))
    return pl.pallas_call(
        _na_kernel,
        grid=(B, n_blocks),
        in_specs=[
            pl.BlockSpec((None, NA_Q, NA_WIDTH), lambda b, i: (b, i, cq)),
            pl.BlockSpec((None, NA_Q, NA_WIDTH), lambda b, i: (b, i, cz)),
            pl.BlockSpec((None, S, NA_WIDTH), lambda b, i: (b, 0, ck)),
            pl.BlockSpec((None, S, NA_WIDTH), lambda b, i: (b, 0, cv)),
            pl.BlockSpec((None, C, NA_WIDTH), lambda b, i: (b, 0, ck)),
            pl.BlockSpec((None, C, NA_WIDTH), lambda b, i: (b, 0, cv)),
            pl.BlockSpec((None, NA_HEADS, NA_Q, NA_K), bias_idx),
            pl.BlockSpec((1, NA_WIDTH), lambda b, i: (0, 0)),
        ],
        out_specs=pl.BlockSpec((None, NA_Q, NA_WIDTH), lambda b, i: (b, i, 0)),
        out_shape=jax.ShapeDtypeStruct((B, S, NA_WIDTH), _BF16),
        name="nbr_attn",
        compiler_params=_params("parallel", "parallel"),
    )(proj, proj, proj, proj, proj_ctx, proj_ctx, bias_tabs, na_g.reshape(1, NA_WIDTH))


def _ctx_attn_kernel(q_ref, z_ref, k_ref, v_ref, g_ref, o_ref):
    o = _softmax_pv(q_ref[...], [(k_ref[...], v_ref[...], None)])
    _branch_out(o, g_ref[...], 1.0, z_ref, o_ref)


def _ctx_attn(proj_ctx, na_g):
    B, C, _ = proj_ctx.shape
    return pl.pallas_call(
        _ctx_attn_kernel,
        grid=(B, NA_HEADS),
        in_specs=[
            pl.BlockSpec((None, C, HEAD_DIM), lambda b, h: (b, 0, COL_BQ + h)),
            pl.BlockSpec((None, C, HEAD_DIM), lambda b, h: (b, 0, COL_BZ + h)),
            pl.BlockSpec((None, C, HEAD_DIM), lambda b, h: (b, 0, COL_BK + h)),
            pl.BlockSpec((None, C, HEAD_DIM), lambda b, h: (b, 0, COL_BV + h)),
            pl.BlockSpec((None, 1, HEAD_DIM), lambda b, h: (h, 0, 0)),
        ],
        out_specs=pl.BlockSpec((None, C, HEAD_DIM), lambda b, h: (b, 0, h)),
        out_shape=jax.ShapeDtypeStruct((B, C, NA_WIDTH), _BF16),
        name="ctx_attn",
        compiler_params=_params("parallel", "parallel"),
    )(proj_ctx, proj_ctx, proj_ctx, proj_ctx, na_g.reshape(NA_HEADS, 1, HEAD_DIM))


def _mlp_kernel(u_ref, v_ref, z_ref, lng_ref, lnb_ref, ws_ref, bs_ref, g_ref, o_ref):
    n_chunks = u_ref.shape[0] // CHUNK
    for g in range(MLP_GROUPS):
        cols = slice(g * HEAD_DIM, (g + 1) * HEAD_DIM)
        w = ws_ref[g].astype(_BF16)
        bs = bs_ref[g]
        ln_g, ln_b, out_g = lng_ref[:, cols], lnb_ref[:, cols], g_ref[:, cols]
        for c in range(n_chunks):
            rows = slice(c * CHUNK, (c + 1) * CHUNK)
            v = v_ref[rows, cols].astype(_F32)
            vc = v - jnp.mean(v, axis=-1, keepdims=True)
            vn = vc * lax.rsqrt(jnp.mean(vc * vc, axis=-1, keepdims=True) + EPS) * ln_g + ln_b
            mixed = jnp.dot(w, vn.astype(_BF16), preferred_element_type=_F32) + bs
            om = u_ref[rows, cols].astype(_F32) * mixed
            z = z_ref[rows, cols].astype(_F32)
            o_ref[rows, cols] = (_rms(om, out_g) * _silu(z)).astype(_BF16)


def _gmlp(proj, ln_g, ln_b, ws, bs, out_g, *, rows_per_step):
    B, S, _ = proj.shape
    return pl.pallas_call(
        _mlp_kernel,
        grid=(B, S // rows_per_step),
        in_specs=[
            pl.BlockSpec((None, rows_per_step, MLP_WIDTH), lambda b, i: (b, i, COL_CU)),
            pl.BlockSpec((None, rows_per_step, MLP_WIDTH), lambda b, i: (b, i, COL_CV)),
            pl.BlockSpec((None, rows_per_step, MLP_WIDTH), lambda b, i: (b, i, COL_CZ)),
            pl.BlockSpec((1, MLP_WIDTH), lambda b, i: (0, 0)),
            pl.BlockSpec((1, MLP_WIDTH), lambda b, i: (0, 0)),
            pl.BlockSpec((MLP_GROUPS, CHUNK, CHUNK), lambda b, i: (0, 0, 0)),
            pl.BlockSpec((MLP_GROUPS, CHUNK, 1), lambda b, i: (0, 0, 0)),
            pl.BlockSpec((1, MLP_WIDTH), lambda b, i: (0, 0)),
        ],
        out_specs=pl.BlockSpec((None, rows_per_step, MLP_WIDTH), lambda b, i: (b, i, 0)),
        out_shape=jax.ShapeDtypeStruct((B, S, MLP_WIDTH), _BF16),
        name="gmlp",
        compiler_params=_params("parallel", "parallel"),
    )(proj, proj, proj, ln_g.reshape(1, MLP_WIDTH), ln_b.reshape(1, MLP_WIDTH), ws,
      bs.reshape(MLP_GROUPS, CHUNK, 1), out_g.reshape(1, MLP_WIDTH))


def _out_kernel(ya_ref, yn_ref, ym_ref, w_ref, x_ref, gate_ref, g_ref, o_ref):
    y = jnp.dot(ya_ref[...], w_ref[:DA_WIDTH, :], preferred_element_type=_F32)
    y = y + jnp.dot(yn_ref[...], w_ref[DA_WIDTH:DA_WIDTH + NA_WIDTH, :], preferred_element_type=_F32)
    y = y + jnp.dot(ym_ref[...], w_ref[DA_WIDTH + NA_WIDTH:, :], preferred_element_type=_F32)
    o_ref[...] = x_ref[...] + gate_ref[...] * _rms(y, g_ref[...])


def _out_proj(ya, yn, ym, w_bf16, x2d, gate, post_g, *, tm):
    T, D = x2d.shape
    nb = gate.shape[0]
    tiles_per_mod = (T // nb) // tm
    return pl.pallas_call(
        _out_kernel,
        grid=(T // tm,),
        in_specs=[
            pl.BlockSpec((tm, DA_WIDTH), lambda i: (i, 0)),
            pl.BlockSpec((tm, NA_WIDTH), lambda i: (i, 0)),
            pl.BlockSpec((tm, MLP_WIDTH), lambda i: (i, 0)),
            pl.BlockSpec((MIX_WIDTH, D), lambda i: (0, 0)),
            pl.BlockSpec((tm, D), lambda i: (i, 0)),
            pl.BlockSpec((None, 1, D), lambda i: (i // tiles_per_mod, 0, 0)),
            pl.BlockSpec((1, D), lambda i: (0, 0)),
        ],
        out_specs=pl.BlockSpec((tm, D), lambda i: (i, 0)),
        out_shape=jax.ShapeDtypeStruct((T, D), _F32),
        name="out_proj",
        compiler_params=_params("parallel"),
    )(ya.reshape(T, DA_WIDTH), yn.reshape(T, NA_WIDTH), ym.reshape(T, MLP_WIDTH), w_bf16, x2d,
      gate, post_g.reshape(1, D))


def _rope_tables(seq):
    n = DA_QK_DIM // 4
    freqs = ROPE_BASE ** (-jnp.arange(n, dtype=_F32) / n)
    pos = jnp.arange(seq)
    row = (pos // GRID_W).astype(_F32)[:, None] * freqs
    col = (pos % GRID_W).astype(_F32)[:, None] * freqs
    zeros = jnp.zeros((seq, n), _F32)

    def one_map(fn_first, fn_second):
        return jnp.concatenate([fn_first(row), fn_second(row), fn_first(col), fn_second(col)], axis=-1)

    cos = one_map(jnp.cos, jnp.cos)
    sa = one_map(lambda a: -jnp.sin(a), lambda a: zeros)
    sb = one_map(lambda a: zeros, jnp.sin)
    return tuple(jnp.concatenate([t, t], axis=-1) for t in (cos, sa, sb))


def _identity_rope_tables(seq):
    return (jnp.ones((seq, HEAD_DIM), _F32), jnp.zeros((seq, HEAD_DIM), _F32),
            jnp.zeros((seq, HEAD_DIM), _F32))


def kernel(x, c, ctx, c_ctx, ada_w, ada_b, pre_g, post_g, w_in, w_out, lam_q1, lam_k1, lam_q2, lam_k2,
           da_g, na_rpb, na_g, mlp_ln_g, mlp_ln_b, mlp_ws, mlp_bs, mlp_g):
    B, S, D = x.shape
    C = ctx.shape[1]
    L = ada_w.shape[0]
    assert S % NA_Q == 0 and S // GRID_W >= NA_KEY_ROWS and C % CHUNK == 0

    cc = jnp.concatenate([c, c_ctx[None], jnp.zeros((8 - B - 1, D), _F32)], axis=0)
    mod = _ada(cc, ada_w, ada_b)

    rope = _rope_tables(S)
    rope_id = _identity_rope_tables(C)
    x2d = x.reshape(B * S, D)
    xc2d = ctx.reshape(B * C, D)

    for l in range(L):
        lam_init = 0.8 - 0.6 * math.exp(-0.3 * l)
        ctx_out = l < L - 1
        shift, scale, gate = (mod[l, :B, i * D:(i + 1) * D].reshape(B, 1, D) for i in range(3))
        shift_c, scale_c, gate_c = (mod[l, B:B + 1, i * D:(i + 1) * D].reshape(1, 1, D) for i in range(3))
        w_in_b = w_in[l].astype(_BF16)
        w_out_b = w_out[l].astype(_BF16)
        lam_params = jnp.stack([lam_q1[l], lam_k1[l], lam_q2[l], lam_k2[l]])

        proj = _in_proj(x2d, pre_g[l], scale, shift, w_in_b, *rope, tm=512, seq=S).reshape(B, S, IN_WIDTH)
        proj_c = _in_proj(xc2d, pre_g[l], scale_c, shift_c, w_in_b, *rope_id, tm=C, seq=C).reshape(B, C, IN_WIDTH)

        ya = _diff_attn(proj, [proj, proj_c], lam_params, da_g[l], lam_init=lam_init)
        yn = _nbr_attn(proj, proj_c, _na_bias_tables(na_rpb[l], S // GRID_W), na_g[l])
        ym = _gmlp(proj, mlp_ln_g[l], mlp_ln_b[l], mlp_ws[l], mlp_bs[l], mlp_g[l], rows_per_step=512)
        x2d_new = _out_proj(ya, yn, ym, w_out_b, x2d, gate, post_g[l], tm=256)

        if ctx_out:
            yca = _diff_attn(proj_c, [proj_c], lam_params, da_g[l], lam_init=lam_init)
            ycn = _ctx_attn(proj_c, na_g[l])
            ycm = _gmlp(proj_c, mlp_ln_g[l], mlp_ln_b[l], mlp_ws[l], mlp_bs[l], mlp_g[l], rows_per_step=C)
            xc2d = _out_proj(yca, ycn, ycm, w_out_b, xc2d, gate_c, post_g[l], tm=C)
        x2d = x2d_new

    return x2d.reshape(B, S, D)
```

```python
import functools
import math

import numpy as np
import jax
import jax.numpy as jnp
from jax import lax
from jax.experimental import pallas as pl
from jax.experimental.pallas import tpu as pltpu

D_MODEL = 2048
DEPTH = 2
GRID_W = 64
HEAD_DIM = 128
DA_HEADS = 6
NA_HEADS = 6
MLP_GROUPS = 4
DA_WIDTH = DA_HEADS * HEAD_DIM
NA_WIDTH = NA_HEADS * HEAD_DIM
MLP_WIDTH = MLP_GROUPS * HEAD_DIM
DA_QK_DIM = HEAD_DIM // 2
NA_WIN_H = 8
NA_WIN_W = 16
CHUNK = 128
ROPE_BASE = 10000.0
EPS = 1e-6
IN_WIDTH = 4 * DA_WIDTH + 4 * NA_WIDTH + 3 * MLP_WIDTH
MIX_WIDTH = DA_WIDTH + NA_WIDTH + MLP_WIDTH

COL_AQ, COL_AK, COL_AV, COL_AZ = 0, 6, 12, 18
COL_BQ, COL_BK, COL_BV, COL_BZ = 24, 30, 36, 42
COL_CU, COL_CV, COL_CZ = 12, 13, 14

DA_SCALE = DA_QK_DIM ** -0.5 * math.log2(math.e)
NA_SCALE = HEAD_DIM ** -0.5
DA_KEY_CHUNK = 512
DA_Q_TILE = 128
DA_TILES_PER_STEP = 8


def _in_segments():
    kinds = ["rope_scale", "rope", "plain", "plain", "na_scale", "plain", "plain", "plain", "gelu", "gelu", "plain"]
    widths = [DA_WIDTH] * 4 + [NA_WIDTH] * 4 + [MLP_WIDTH] * 3
    segs, c0 = [], 0
    for width, kind in zip(widths, kinds):
        segs.append((c0, width, kind))
        c0 += width
    return tuple(segs)


IN_SEGMENTS = _in_segments()
IN_TILE_ROWS = 256
OUT_TILE_ROWS = 512
OUT_SUB_ROWS = 256

NA_BLOCK_ROWS = 4
NA_KEY_ROWS = NA_BLOCK_ROWS + NA_WIN_H
NA_Q = NA_BLOCK_ROWS * GRID_W
NA_K = NA_KEY_ROWS * GRID_W
MASK_VALUE = -1e30

VMEM_LIMIT_BYTES = 56 * 1024 * 1024

_BF16 = jnp.bfloat16
_F32 = jnp.float32


def _silu(x):
    return x * (1.0 / (1.0 + jnp.exp(-x)))


def _gelu_tanh(x):
    c = math.sqrt(2.0 / math.pi)
    return 0.5 * x * (1.0 + jnp.tanh(c * (x + 0.044715 * (x * x * x))))


def _rms(x, g):
    return x * lax.rsqrt(jnp.mean(x * x, axis=-1, keepdims=True) + EPS) * g


def _dot_nt(a, b):
    return lax.dot_general(a, b, (((1,), (1,)), ((), ())), preferred_element_type=_F32)


def _params(*sem):
    return pltpu.CompilerParams(dimension_semantics=sem, vmem_limit_bytes=VMEM_LIMIT_BYTES)


def _ada_kernel(c_ref, w_ref, b_ref, o_ref):
    sc = _silu(c_ref[...]).astype(_BF16)
    o_ref[...] = jnp.dot(sc, w_ref[...].astype(_BF16), preferred_element_type=_F32) + b_ref[...]


def _ada(cc, ada_w, ada_b):
    L, D, N = ada_w.shape
    R = cc.shape[0]
    tn = 768
    return pl.pallas_call(
        _ada_kernel,
        grid=(L, N // tn),
        in_specs=[
            pl.BlockSpec((R, D), lambda l, j: (0, 0)),
            pl.BlockSpec((None, D, tn), lambda l, j: (l, 0, j)),
            pl.BlockSpec((None, 1, tn), lambda l, j: (l, 0, j)),
        ],
        out_specs=pl.BlockSpec((None, R, tn), lambda l, j: (l, 0, j)),
        out_shape=jax.ShapeDtypeStruct((L, R, N), _F32),
        name="ada_mod",
        compiler_params=_params("parallel", "parallel"),
    )(cc, ada_w, ada_b.reshape(L, 1, N))


def _in_kernel(x_ref, g_ref, scale_ref, shift_ref, w_ref, cos_ref, sa_ref, sb_ref, o_ref):
    x = x_ref[...]
    h = (_rms(x, g_ref[...]) * (1.0 + scale_ref[...]) + shift_ref[...]).astype(_BF16)
    cos, sa, sb = cos_ref[...], sa_ref[...], sb_ref[...]
    for c0, width, epilogue in IN_SEGMENTS:
        acc = jnp.dot(h, w_ref[:, c0:c0 + width], preferred_element_type=_F32)
        if epilogue in ("rope", "rope_scale"):
            for hh in range(width // HEAD_DIM):
                xs = acc[:, hh * HEAD_DIM:(hh + 1) * HEAD_DIM]
                r = xs * cos + pltpu.roll(xs, HEAD_DIM - 16, 1) * sa + pltpu.roll(xs, 16, 1) * sb
                if epilogue == "rope_scale":
                    r = r * DA_SCALE
                o_ref[:, c0 + hh * HEAD_DIM:c0 + (hh + 1) * HEAD_DIM] = r.astype(_BF16)
        elif epilogue == "na_scale":
            o_ref[:, c0:c0 + width] = (acc * NA_SCALE).astype(_BF16)
        elif epilogue == "gelu":
            o_ref[:, c0:c0 + width] = _gelu_tanh(acc).astype(_BF16)
        else:
            o_ref[:, c0:c0 + width] = acc.astype(_BF16)


def _in_proj(x2d, pre_g, scale, shift, w_bf16, layer, cos, sa, sb, *, seq):
    T, D = x2d.shape
    nb = scale.shape[0]
    tm = IN_TILE_ROWS
    tiles_per_mod = (T // nb) // tm
    tiles_per_seq = seq // tm
    return pl.pallas_call(
        _in_kernel,
        grid=(T // tm,),
        in_specs=[
            pl.BlockSpec((tm, D), lambda i: (i, 0)),
            pl.BlockSpec((1, D), lambda i: (0, 0)),
            pl.BlockSpec((None, 1, D), lambda i: (i // tiles_per_mod, 0, 0)),
            pl.BlockSpec((None, 1, D), lambda i: (i // tiles_per_mod, 0, 0)),
            pl.BlockSpec((None, D, IN_WIDTH), lambda i: (layer, 0, 0), pipeline_mode=pl.Buffered(1)),
            pl.BlockSpec((tm, HEAD_DIM), lambda i: (i % tiles_per_seq, 0)),
            pl.BlockSpec((tm, HEAD_DIM), lambda i: (i % tiles_per_seq, 0)),
            pl.BlockSpec((tm, HEAD_DIM), lambda i: (i % tiles_per_seq, 0)),
        ],
        out_specs=pl.BlockSpec((tm, IN_WIDTH), lambda i: (i, 0)),
        out_shape=jax.ShapeDtypeStruct((T, IN_WIDTH), _BF16),
        name="in_proj",
        compiler_params=_params("parallel"),
    )(x2d, pre_g.reshape(1, D), scale, shift, w_bf16, cos, sa, sb)


def _softmax_pv(qq, sources):
    scores = []
    for k, _, bias in sources:
        s = _dot_nt(qq, k)
        if bias is not None:
            s = s + bias
        scores.append(s)
    m = jnp.max(scores[0], axis=-1, keepdims=True)
    for s in scores[1:]:
        m = jnp.maximum(m, jnp.max(s, axis=-1, keepdims=True))
    l = None
    o = None
    for s, (_, v, _) in zip(scores, sources):
        p = jnp.exp(s - m)
        ps = jnp.sum(p, axis=-1, keepdims=True)
        pv = jnp.dot(p.astype(_BF16), v, preferred_element_type=_F32)
        l = ps if l is None else l + ps
        o = pv if o is None else o + pv
    return o / l


def _branch_out(o, g, factor, z_ref, o_ref):
    z = z_ref[...].astype(_F32)
    o_ref[...] = (_rms(o, g) * factor * _silu(z)).astype(_BF16)


def _da_kernel(*refs, n_src, lam_init):
    q_ref, z_ref = refs[0], refs[1]
    kv_refs = refs[2:2 + 2 * n_src]
    lam_ref, g_ref, o_ref, s_ref = refs[2 + 2 * n_src:]
    tq = DA_Q_TILE
    n_tiles = q_ref.shape[0] // tq

    chunks = []
    col = 0
    for i in range(n_src):
        k_ref, v_ref = kv_refs[2 * i], kv_refs[2 * i + 1]
        size = k_ref.shape[0]
        ck = min(size, DA_KEY_CHUNK)
        for c in range(size // ck):
            chunks.append((k_ref, v_ref, c * ck, ck, col))
            col += ck

    lp = lam_ref[...]
    lam = (jnp.exp(jnp.sum(lp[0:1] * lp[1:2], axis=-1, keepdims=True))
           - jnp.exp(jnp.sum(lp[2:3] * lp[3:4], axis=-1, keepdims=True)) + lam_init)

    def scores_and_max(t):
        q = q_ref[t * tq:(t + 1) * tq, :]
        lane = lax.broadcasted_iota(jnp.int32, q.shape, 1)
        zero = jnp.zeros_like(q)
        qq = jnp.concatenate([jnp.where(lane < DA_QK_DIM, q, zero),
                              jnp.where(lane >= DA_QK_DIM, q, zero)], axis=0)
        mx = None
        for k_ref, _, r0, ck, c0 in chunks:
            s = _dot_nt(qq, k_ref[r0:r0 + ck, :])
            s_ref[t % 2, :, c0:c0 + ck] = s
            for jj in range(ck // HEAD_DIM):
                part = s[:, jj * HEAD_DIM:(jj + 1) * HEAD_DIM]
                mx = part if mx is None else jnp.maximum(mx, part)
        return jnp.max(mx, axis=-1, keepdims=True)

    def softmax_pv_out(t, m):
        acc = None
        for _, v_ref, r0, ck, c0 in chunks:
            p = jnp.exp2(s_ref[t % 2, :, c0:c0 + ck] - m).astype(_BF16)
            v_aug = jnp.concatenate([v_ref[r0:r0 + ck, :], jnp.ones((ck, HEAD_DIM), _BF16)], axis=1)
            pv = jnp.dot(p, v_aug, preferred_element_type=_F32)
            acc = pv if acc is None else acc + pv
        o = acc[:, :HEAD_DIM] / acc[:, HEAD_DIM:]
        oa = o[:tq] - lam * o[tq:]
        rows = slice(t * tq, (t + 1) * tq)
        z = z_ref[rows, :].astype(_F32)
        o_ref[rows, :] = (_rms(oa, g_ref[...]) * (1.0 - lam_init) * _silu(z)).astype(_BF16)

    row_max = scores_and_max(0)
    for t in range(1, n_tiles):
        next_max = scores_and_max(t)
        softmax_pv_out(t - 1, row_max)
        row_max = next_max
    softmax_pv_out(n_tiles - 1, row_max)


def _diff_attn(q_proj, kv_projs, lam_params, da_g, *, lam_init):
    B, Sq, _ = q_proj.shape
    tq = min(DA_TILES_PER_STEP * DA_Q_TILE, Sq)
    in_specs = [
        pl.BlockSpec((None, tq, HEAD_DIM), lambda b, h, i: (b, i, COL_AQ + h)),
        pl.BlockSpec((None, tq, HEAD_DIM), lambda b, h, i: (b, i, COL_AZ + h)),
    ]
    args = [q_proj, q_proj]
    for kp in kv_projs:
        sk = kp.shape[1]
        in_specs.append(pl.BlockSpec((None, sk, HEAD_DIM), lambda b, h, i: (b, 0, COL_AK + h)))
        in_specs.append(pl.BlockSpec((None, sk, HEAD_DIM), lambda b, h, i: (b, 0, COL_AV + h)))
        args += [kp, kp]
    in_specs += [
        pl.BlockSpec((4, DA_QK_DIM), lambda b, h, i: (0, 0)),
        pl.BlockSpec((1, HEAD_DIM), lambda b, h, i: (0, 0)),
    ]
    args += [lam_params, da_g.reshape(1, HEAD_DIM)]
    return pl.pallas_call(
        functools.partial(_da_kernel, n_src=len(kv_projs), lam_init=lam_init),
        grid=(B, DA_HEADS, Sq // tq),
        in_specs=in_specs,
        out_specs=pl.BlockSpec((None, tq, HEAD_DIM), lambda b, h, i: (b, i, h)),
        out_shape=jax.ShapeDtypeStruct((B, Sq, DA_WIDTH), _BF16),
        scratch_shapes=[pltpu.VMEM((2, 2 * DA_Q_TILE, sum(kp.shape[1] for kp in kv_projs)), _F32)],
        name="diff_attn",
        compiler_params=_params("parallel", "parallel", "parallel"),
    )(*args)


def _na_kernel(q_ref, z_ref, k_ref, v_ref, kc_ref, vc_ref, bias_ref, g_ref, o_ref):
    rb = pl.program_id(1)
    rows = k_ref.shape[0] // GRID_W
    kr0 = jnp.clip(rb * NA_BLOCK_ROWS - NA_WIN_H // 2, 0, rows - NA_KEY_ROWS)
    start = pl.multiple_of(kr0 * GRID_W, GRID_W)
    for h in range(NA_HEADS):
        cols = slice(h * HEAD_DIM, (h + 1) * HEAD_DIM)
        k_win = k_ref[pl.ds(start, NA_K), cols]
        v_win = v_ref[pl.ds(start, NA_K), cols]
        o = _softmax_pv(q_ref[:, cols], [(k_win, v_win, bias_ref[h]), (kc_ref[:, cols], vc_ref[:, cols], None)])
        z = z_ref[:, cols].astype(_F32)
        o_ref[:, cols] = (_rms(o, g_ref[:, cols]) * _silu(z)).astype(_BF16)


def _na_bias_kernel(r_ref, o_ref, *, rows):
    shape = (GRID_W, 2 * GRID_W)
    c = lax.broadcasted_iota(jnp.int32, shape, 0)
    lane = lax.broadcasted_iota(jnp.int32, shape, 1)
    kc = lane & (GRID_W - 1)
    cs = jnp.clip(c - NA_WIN_W // 2, 0, GRID_W - NA_WIN_W)
    col_valid = (kc >= cs) & (kc < cs + NA_WIN_W)
    first_key_row = lane < GRID_W
    toeplitz = []
    for t in range(2 * NA_WIN_H):
        row = jnp.broadcast_to(r_ref[t:t + 1, :], shape)
        x = pltpu.roll(row, 2 * GRID_W - (NA_WIN_W - 1), 1, stride=1, stride_axis=0)
        toeplitz.append(jnp.where(col_valid, x, MASK_VALUE))
    masked = jnp.full(shape, MASK_VALUE, _F32)
    for cfg, (r0, kr0) in enumerate(_na_block_configs(rows)):
        for ri in range(NA_BLOCK_ROWS):
            r = r0 + ri
            rs = min(max(r - NA_WIN_H // 2, 0), rows - NA_WIN_H)
            for j in range(NA_KEY_ROWS // 2):
                kra = kr0 + 2 * j
                valid_a = rs <= kra < rs + NA_WIN_H
                valid_b = rs <= kra + 1 < rs + NA_WIN_H
                t = kra - r + NA_WIN_H
                if valid_a and valid_b:
                    blk = toeplitz[t]
                elif valid_a:
                    blk = jnp.where(first_key_row, toeplitz[t], MASK_VALUE)
                elif valid_b:
                    blk = jnp.where(first_key_row, MASK_VALUE, toeplitz[t])
                else:
                    blk = masked
                o_ref[cfg, ri * GRID_W:(ri + 1) * GRID_W, j * 2 * GRID_W:(j + 1) * 2 * GRID_W] = blk


def _na_block_configs(rows):
    return ((0, 0), (2 * NA_BLOCK_ROWS, 2 * NA_BLOCK_ROWS - NA_WIN_H // 2),
            (rows - NA_BLOCK_ROWS, rows - NA_KEY_ROWS))


def _na_bias_tables(rpb, rows):
    H, nr, ncol = rpb.shape
    lo = jnp.pad(rpb, ((0, 0), (1, 0), (0, 2 * GRID_W - ncol)))
    hi = jnp.pad(rpb, ((0, 0), (0, 1), (GRID_W, GRID_W - ncol)))
    return pl.pallas_call(
        functools.partial(_na_bias_kernel, rows=rows),
        grid=(H,),
        in_specs=[pl.BlockSpec((None, nr + 1, 2 * GRID_W), lambda h: (h, 0, 0))],
        out_specs=pl.BlockSpec((3, None, NA_Q, NA_K), lambda h: (0, h, 0, 0)),
        out_shape=jax.ShapeDtypeStruct((3, H, NA_Q, NA_K), _F32),
        name="na_bias",
        compiler_params=_params("parallel"),
    )(lo + hi)


def _nbr_attn(proj, proj_ctx, bias_tabs, na_g):
    B, S, _ = proj.shape
    C = proj_ctx.shape[1]
    n_blocks = S // NA_Q

    def bias_idx(b, i):
        return (jnp.where(i == 0, 0, jnp.where(i == n_blocks - 1, 2, 1)), 0, 0, 0)

    cq, ck, cv, cz = (c * HEAD_DIM // NA_WIDTH for c in (COL_BQ, COL_BK, COL_BV, COL_BZ))
    return pl.pallas_call(
        _na_kernel,
        grid=(B, n_blocks),
        in_specs=[
            pl.BlockSpec((None, NA_Q, NA_WIDTH), lambda b, i: (b, i, cq)),
            pl.BlockSpec((None, NA_Q, NA_WIDTH), lambda b, i: (b, i, cz)),
            pl.BlockSpec((None, S, NA_WIDTH), lambda b, i: (b, 0, ck)),
            pl.BlockSpec((None, S, NA_WIDTH), lambda b, i: (b, 0, cv)),
            pl.BlockSpec((None, C, NA_WIDTH), lambda b, i: (b, 0, ck)),
            pl.BlockSpec((None, C, NA_WIDTH), lambda b, i: (b, 0, cv)),
            pl.BlockSpec((None, NA_HEADS, NA_Q, NA_K), bias_idx),
            pl.BlockSpec((1, NA_WIDTH), lambda b, i: (0, 0)),
        ],
        out_specs=pl.BlockSpec((None, NA_Q, NA_WIDTH), lambda b, i: (b, i, 0)),
        out_shape=jax.ShapeDtypeStruct((B, S, NA_WIDTH), _BF16),
        name="nbr_attn",
        compiler_params=_params("parallel", "parallel"),
    )(proj, proj, proj, proj, proj_ctx, proj_ctx, bias_tabs, na_g.reshape(1, NA_WIDTH))


def _ctx_attn_kernel(q_ref, z_ref, k_ref, v_ref, g_ref, o_ref):
    o = _softmax_pv(q_ref[...], [(k_ref[...], v_ref[...], None)])
    _branch_out(o, g_ref[...], 1.0, z_ref, o_ref)


def _ctx_attn(proj_ctx, na_g):
    B, C, _ = proj_ctx.shape
    return pl.pallas_call(
        _ctx_attn_kernel,
        grid=(B, NA_HEADS),
        in_specs=[
            pl.BlockSpec((None, C, HEAD_DIM), lambda b, h: (b, 0, COL_BQ + h)),
            pl.BlockSpec((None, C, HEAD_DIM), lambda b, h: (b, 0, COL_BZ + h)),
            pl.BlockSpec((None, C, HEAD_DIM), lambda b, h: (b, 0, COL_BK + h)),
            pl.BlockSpec((None, C, HEAD_DIM), lambda b, h: (b, 0, COL_BV + h)),
            pl.BlockSpec((None, 1, HEAD_DIM), lambda b, h: (h, 0, 0)),
        ],
        out_specs=pl.BlockSpec((None, C, HEAD_DIM), lambda b, h: (b, 0, h)),
        out_shape=jax.ShapeDtypeStruct((B, C, NA_WIDTH), _BF16),
        name="ctx_attn",
        compiler_params=_params("parallel", "parallel"),
    )(proj_ctx, proj_ctx, proj_ctx, proj_ctx, na_g.reshape(NA_HEADS, 1, HEAD_DIM))


def _mlp_kernel(u_ref, v_ref, z_ref, lng_ref, lnb_ref, ws_ref, bs_ref, g_ref, o_ref):
    n_chunks = u_ref.shape[0] // CHUNK
    for g in range(MLP_GROUPS):
        cols = slice(g * HEAD_DIM, (g + 1) * HEAD_DIM)
        w = ws_ref[g].astype(_BF16)
        bs = bs_ref[g]
        ln_g, ln_b, out_g = lng_ref[:, cols], lnb_ref[:, cols], g_ref[:, cols]
        for c in range(n_chunks):
            rows = slice(c * CHUNK, (c + 1) * CHUNK)
            v = v_ref[rows, cols].astype(_F32)
            vc = v - jnp.mean(v, axis=-1, keepdims=True)
            vn = vc * lax.rsqrt(jnp.mean(vc * vc, axis=-1, keepdims=True) + EPS) * ln_g + ln_b
            mixed = jnp.dot(w, vn.astype(_BF16), preferred_element_type=_F32) + bs
            om = u_ref[rows, cols].astype(_F32) * mixed
            z = z_ref[rows, cols].astype(_F32)
            o_ref[rows, cols] = (_rms(om, out_g) * _silu(z)).astype(_BF16)


def _gmlp(proj, ln_g, ln_b, ws, bs, out_g, *, rows_per_step):
    B, S, _ = proj.shape
    return pl.pallas_call(
        _mlp_kernel,
        grid=(B, S // rows_per_step),
        in_specs=[
            pl.BlockSpec((None, rows_per_step, MLP_WIDTH), lambda b, i: (b, i, COL_CU)),
            pl.BlockSpec((None, rows_per_step, MLP_WIDTH), lambda b, i: (b, i, COL_CV)),
            pl.BlockSpec((None, rows_per_step, MLP_WIDTH), lambda b, i: (b, i, COL_CZ)),
            pl.BlockSpec((1, MLP_WIDTH), lambda b, i: (0, 0)),
            pl.BlockSpec((1, MLP_WIDTH), lambda b, i: (0, 0)),
            pl.BlockSpec((MLP_GROUPS, CHUNK, CHUNK), lambda b, i: (0, 0, 0)),
            pl.BlockSpec((MLP_GROUPS, CHUNK, 1), lambda b, i: (0, 0, 0)),
            pl.BlockSpec((1, MLP_WIDTH), lambda b, i: (0, 0)),
        ],
        out_specs=pl.BlockSpec((None, rows_per_step, MLP_WIDTH), lambda b, i: (b, i, 0)),
        out_shape=jax.ShapeDtypeStruct((B, S, MLP_WIDTH), _BF16),
        name="gmlp",
        compiler_params=_params("parallel", "parallel"),
    )(proj, proj, proj, ln_g.reshape(1, MLP_WIDTH), ln_b.reshape(1, MLP_WIDTH), ws,
      bs.reshape(MLP_GROUPS, CHUNK, 1), out_g.reshape(1, MLP_WIDTH))


def _out_kernel(ya_ref, yn_ref, ym_ref, w_ref, x_ref, gate_ref, g_ref, o_ref):
    for r0 in range(0, x_ref.shape[0], OUT_SUB_ROWS):
        rows = slice(r0, r0 + OUT_SUB_ROWS)
        y = jnp.dot(ya_ref[rows, :], w_ref[:DA_WIDTH, :], preferred_element_type=_F32)
        y = y + jnp.dot(yn_ref[rows, :], w_ref[DA_WIDTH:DA_WIDTH + NA_WIDTH, :], preferred_element_type=_F32)
        y = y + jnp.dot(ym_ref[rows, :], w_ref[DA_WIDTH + NA_WIDTH:, :], preferred_element_type=_F32)
        o_ref[rows, :] = x_ref[rows, :] + gate_ref[...] * _rms(y, g_ref[...])


def _out_proj(ya, yn, ym, w_bf16, layer, x2d, gate, post_g):
    T, D = x2d.shape
    nb = gate.shape[0]
    tm = min(OUT_TILE_ROWS, T // nb)
    tiles_per_mod = (T // nb) // tm
    return pl.pallas_call(
        _out_kernel,
        grid=(T // tm,),
        in_specs=[
            pl.BlockSpec((tm, DA_WIDTH), lambda i: (i, 0)),
            pl.BlockSpec((tm, NA_WIDTH), lambda i: (i, 0)),
            pl.BlockSpec((tm, MLP_WIDTH), lambda i: (i, 0)),
            pl.BlockSpec((None, MIX_WIDTH, D), lambda i: (layer, 0, 0), pipeline_mode=pl.Buffered(1)),
            pl.BlockSpec((tm, D), lambda i: (i, 0)),
            pl.BlockSpec((None, 1, D), lambda i: (i // tiles_per_mod, 0, 0)),
            pl.BlockSpec((1, D), lambda i: (0, 0)),
        ],
        out_specs=pl.BlockSpec((tm, D), lambda i: (i, 0)),
        out_shape=jax.ShapeDtypeStruct((T, D), _F32),
        name="out_proj",
        compiler_params=_params("parallel"),
    )(ya.reshape(T, DA_WIDTH), yn.reshape(T, NA_WIDTH), ym.reshape(T, MLP_WIDTH), w_bf16, x2d,
      gate, post_g.reshape(1, D))


def _rope_tables(seq):
    n = DA_QK_DIM // 4
    freqs = ROPE_BASE ** (-np.arange(n, dtype=np.float64) / n)
    pos = np.arange(seq)
    row = (pos // GRID_W).astype(np.float64)[:, None] * freqs
    col = (pos % GRID_W).astype(np.float64)[:, None] * freqs
    zeros = np.zeros((seq, n))

    def one_map(fn_first, fn_second):
        return np.concatenate([fn_first(row), fn_second(row), fn_first(col), fn_second(col)], axis=-1)

    cos = one_map(np.cos, np.cos)
    sa = one_map(lambda a: -np.sin(a), lambda a: zeros)
    sb = one_map(lambda a: zeros, np.sin)
    return tuple(jnp.asarray(np.concatenate([t, t], axis=-1), _F32) for t in (cos, sa, sb))


def _identity_rope_tables(seq):
    return (jnp.ones((seq, HEAD_DIM), _F32), jnp.zeros((seq, HEAD_DIM), _F32),
            jnp.zeros((seq, HEAD_DIM), _F32))


def kernel(x, c, ctx, c_ctx, ada_w, ada_b, pre_g, post_g, w_in, w_out, lam_q1, lam_k1, lam_q2, lam_k2,
           da_g, na_rpb, na_g, mlp_ln_g, mlp_ln_b, mlp_ws, mlp_bs, mlp_g):
    B, S, D = x.shape
    C = ctx.shape[1]
    L = ada_w.shape[0]
    assert S % NA_Q == 0 and S // GRID_W >= NA_KEY_ROWS and C % CHUNK == 0

    cc = jnp.concatenate([c, c_ctx[None], jnp.zeros((8 - B - 1, D), _F32)], axis=0)
    mod = _ada(cc, ada_w, ada_b)

    rope = _rope_tables(S)
    rope_id = _identity_rope_tables(C)
    w_in_b = w_in.astype(_BF16)
    w_out_b = w_out.astype(_BF16)
    x2d = x.reshape(B * S, D)
    xc2d = ctx.reshape(B * C, D)

    for l in range(L):
        lam_init = 0.8 - 0.6 * math.exp(-0.3 * l)
        ctx_out = l < L - 1
        shift, scale, gate = (mod[l, :B, i * D:(i + 1) * D].reshape(B, 1, D) for i in range(3))
        shift_c, scale_c, gate_c = (mod[l, B:B + 1, i * D:(i + 1) * D].reshape(1, 1, D) for i in range(3))
        lam_params = jnp.stack([lam_q1[l], lam_k1[l], lam_q2[l], lam_k2[l]])

        proj = _in_proj(x2d, pre_g[l], scale, shift, w_in_b, l, *rope, seq=S).reshape(B, S, IN_WIDTH)
        proj_c = _in_proj(xc2d, pre_g[l], scale_c, shift_c, w_in_b, l, *rope_id, seq=C).reshape(B, C, IN_WIDTH)

        ya = _diff_attn(proj, [proj, proj_c], lam_params, da_g[l], lam_init=lam_init)
        yn = _nbr_attn(proj, proj_c, _na_bias_tables(na_rpb[l], S // GRID_W), na_g[l])
        ym = _gmlp(proj, mlp_ln_g[l], mlp_ln_b[l], mlp_ws[l], mlp_bs[l], mlp_g[l], rows_per_step=512)
        x2d_new = _out_proj(ya, yn, ym, w_out_b, l, x2d, gate, post_g[l])

        if ctx_out:
            yca = _diff_attn(proj_c, [proj_c], lam_params, da_g[l], lam_init=lam_init)
            ycn = _ctx_attn(proj_c, na_g[l])
            ycm = _gmlp(proj_c, mlp_ln_g[l], mlp_ln_b[l], mlp_ws[l], mlp_bs[l], mlp_g[l], rows_per_step=C)
            xc2d = _out_proj(yca, ycn, ycm, w_out_b, l, xc2d, gate_c, post_g[l])
        x2d = x2d_new

    return x2d.reshape(B, S, D)
```

```python
import functools
import math

import numpy as np
import jax
import jax.numpy as jnp
from jax import lax
from jax.experimental import pallas as pl
from jax.experimental.pallas import tpu as pltpu

D_MODEL = 2048
DEPTH = 2
GRID_W = 64
HEAD_DIM = 128
DA_HEADS = 6
NA_HEADS = 6
MLP_GROUPS = 4
DA_WIDTH = DA_HEADS * HEAD_DIM
NA_WIDTH = NA_HEADS * HEAD_DIM
MLP_WIDTH = MLP_GROUPS * HEAD_DIM
DA_QK_DIM = HEAD_DIM // 2
NA_WIN_H = 8
NA_WIN_W = 16
CHUNK = 128
ROPE_BASE = 10000.0
EPS = 1e-6
IN_WIDTH = 4 * DA_WIDTH + 4 * NA_WIDTH + 3 * MLP_WIDTH
MIX_WIDTH = DA_WIDTH + NA_WIDTH + MLP_WIDTH

COL_AQ, COL_AK, COL_AV, COL_AZ = 0, 6, 12, 18
COL_BQ, COL_BK, COL_BV, COL_BZ = 24, 30, 36, 42
COL_CU, COL_CV, COL_CZ = 12, 13, 14

LOG2E = math.log2(math.e)
DA_SCALE = DA_QK_DIM ** -0.5 * LOG2E
NA_SCALE = HEAD_DIM ** -0.5 * LOG2E
DA_KEY_CHUNK = 512
DA_Q_TILE = 128
DA_TILES_PER_STEP = 8


def _in_segments():
    kinds = ["rope_scale", "rope", "plain", "plain", "na_scale", "plain", "plain", "plain", "gelu", "gelu", "plain"]
    widths = [DA_WIDTH] * 4 + [NA_WIDTH] * 4 + [MLP_WIDTH] * 3
    segs, c0 = [], 0
    for width, kind in zip(widths, kinds):
        segs.append((c0, width, kind))
        c0 += width
    return tuple(segs)


IN_SEGMENTS = _in_segments()
IN_TILE_ROWS = 256
OUT_TILE_ROWS = 512
OUT_SUB_ROWS = 256

NA_BLOCK_ROWS = 4
NA_KEY_ROWS = NA_BLOCK_ROWS + NA_WIN_H
NA_Q = NA_BLOCK_ROWS * GRID_W
NA_K = NA_KEY_ROWS * GRID_W
MASK_VALUE = -1e30

VMEM_LIMIT_BYTES = 56 * 1024 * 1024

_BF16 = jnp.bfloat16
_F32 = jnp.float32


def _silu(x):
    return x * (1.0 / (1.0 + jnp.exp(-x)))


def _gelu_tanh(x):
    c = math.sqrt(2.0 / math.pi)
    return 0.5 * x * (1.0 + jnp.tanh(c * (x + 0.044715 * (x * x * x))))


def _rms(x, g):
    return x * lax.rsqrt(jnp.mean(x * x, axis=-1, keepdims=True) + EPS) * g


def _dot_nt(a, b):
    return lax.dot_general(a, b, (((1,), (1,)), ((), ())), preferred_element_type=_F32)


def _params(*sem):
    return pltpu.CompilerParams(dimension_semantics=sem, vmem_limit_bytes=VMEM_LIMIT_BYTES)


def _ada_kernel(c_ref, w_ref, b_ref, o_ref):
    sc = _silu(c_ref[...]).astype(_BF16)
    o_ref[...] = jnp.dot(sc, w_ref[...].astype(_BF16), preferred_element_type=_F32) + b_ref[...]


def _ada(cc, ada_w, ada_b):
    L, D, N = ada_w.shape
    R = cc.shape[0]
    tn = 768
    return pl.pallas_call(
        _ada_kernel,
        grid=(L, N // tn),
        in_specs=[
            pl.BlockSpec((R, D), lambda l, j: (0, 0)),
            pl.BlockSpec((None, D, tn), lambda l, j: (l, 0, j)),
            pl.BlockSpec((None, 1, tn), lambda l, j: (l, 0, j)),
        ],
        out_specs=pl.BlockSpec((None, R, tn), lambda l, j: (l, 0, j)),
        out_shape=jax.ShapeDtypeStruct((L, R, N), _F32),
        name="ada_mod",
        compiler_params=_params("parallel", "parallel"),
    )(cc, ada_w, ada_b.reshape(L, 1, N))


def _in_kernel(x_ref, g_ref, scale_ref, shift_ref, w_ref, cos_ref, sa_ref, sb_ref, o_ref):
    x = x_ref[...]
    h = (_rms(x, g_ref[...]) * (1.0 + scale_ref[...]) + shift_ref[...]).astype(_BF16)
    cos, sa, sb = cos_ref[...], sa_ref[...], sb_ref[...]
    for c0, width, epilogue in IN_SEGMENTS:
        acc = jnp.dot(h, w_ref[:, c0:c0 + width], preferred_element_type=_F32)
        if epilogue in ("rope", "rope_scale"):
            for hh in range(width // HEAD_DIM):
                xs = acc[:, hh * HEAD_DIM:(hh + 1) * HEAD_DIM]
                r = xs * cos + pltpu.roll(xs, HEAD_DIM - 16, 1) * sa + pltpu.roll(xs, 16, 1) * sb
                if epilogue == "rope_scale":
                    r = r * DA_SCALE
                o_ref[:, c0 + hh * HEAD_DIM:c0 + (hh + 1) * HEAD_DIM] = r.astype(_BF16)
        elif epilogue == "na_scale":
            o_ref[:, c0:c0 + width] = (acc * NA_SCALE).astype(_BF16)
        elif epilogue == "gelu":
            o_ref[:, c0:c0 + width] = _gelu_tanh(acc).astype(_BF16)
        else:
            o_ref[:, c0:c0 + width] = acc.astype(_BF16)


def _in_proj(x2d, pre_g, scale, shift, w_bf16, layer, cos, sa, sb, *, seq):
    T, D = x2d.shape
    nb = scale.shape[0]
    tm = IN_TILE_ROWS
    tiles_per_mod = (T // nb) // tm
    tiles_per_seq = seq // tm
    return pl.pallas_call(
        _in_kernel,
        grid=(T // tm,),
        in_specs=[
            pl.BlockSpec((tm, D), lambda i: (i, 0)),
            pl.BlockSpec((1, D), lambda i: (0, 0)),
            pl.BlockSpec((None, 1, D), lambda i: (i // tiles_per_mod, 0, 0)),
            pl.BlockSpec((None, 1, D), lambda i: (i // tiles_per_mod, 0, 0)),
            pl.BlockSpec((None, D, IN_WIDTH), lambda i: (layer, 0, 0), pipeline_mode=pl.Buffered(1)),
            pl.BlockSpec((tm, HEAD_DIM), lambda i: (i % tiles_per_seq, 0)),
            pl.BlockSpec((tm, HEAD_DIM), lambda i: (i % tiles_per_seq, 0)),
            pl.BlockSpec((tm, HEAD_DIM), lambda i: (i % tiles_per_seq, 0)),
        ],
        out_specs=pl.BlockSpec((tm, IN_WIDTH), lambda i: (i, 0)),
        out_shape=jax.ShapeDtypeStruct((T, IN_WIDTH), _BF16),
        name="in_proj",
        compiler_params=_params("parallel"),
    )(x2d, pre_g.reshape(1, D), scale, shift, w_bf16, cos, sa, sb)


def _softmax_pv(qq, sources):
    scores = []
    for k, _, bias in sources:
        s = _dot_nt(qq, k)
        if bias is not None:
            s = s + bias
        scores.append(s)
    m = jnp.max(scores[0], axis=-1, keepdims=True)
    for s in scores[1:]:
        m = jnp.maximum(m, jnp.max(s, axis=-1, keepdims=True))
    acc = None
    for s, (_, v, _) in zip(scores, sources):
        p = jnp.exp2(s - m).astype(_BF16)
        v_aug = jnp.concatenate([v, jnp.ones(v.shape, _BF16)], axis=1)
        pv = jnp.dot(p, v_aug, preferred_element_type=_F32)
        acc = pv if acc is None else acc + pv
    return acc[:, :HEAD_DIM] / acc[:, HEAD_DIM:]


def _branch_out(o, g, factor, z_ref, o_ref):
    z = z_ref[...].astype(_F32)
    o_ref[...] = (_rms(o, g) * factor * _silu(z)).astype(_BF16)


def _da_kernel(*refs, n_src, lam_init):
    q_ref, z_ref = refs[0], refs[1]
    kv_refs = refs[2:2 + 2 * n_src]
    lam_ref, g_ref, o_ref, s_ref = refs[2 + 2 * n_src:]
    tq = DA_Q_TILE
    n_tiles = q_ref.shape[0] // tq

    chunks = []
    col = 0
    for i in range(n_src):
        k_ref, v_ref = kv_refs[2 * i], kv_refs[2 * i + 1]
        size = k_ref.shape[0]
        ck = min(size, DA_KEY_CHUNK)
        for c in range(size // ck):
            chunks.append((k_ref, v_ref, c * ck, ck, col))
            col += ck

    lp = lam_ref[...]
    lam = (jnp.exp(jnp.sum(lp[0:1] * lp[1:2], axis=-1, keepdims=True))
           - jnp.exp(jnp.sum(lp[2:3] * lp[3:4], axis=-1, keepdims=True)) + lam_init)

    def scores_and_max(t):
        q = q_ref[t * tq:(t + 1) * tq, :]
        lane = lax.broadcasted_iota(jnp.int32, q.shape, 1)
        zero = jnp.zeros_like(q)
        qq = jnp.concatenate([jnp.where(lane < DA_QK_DIM, q, zero),
                              jnp.where(lane >= DA_QK_DIM, q, zero)], axis=0)
        mx = None
        for k_ref, _, r0, ck, c0 in chunks:
            s = _dot_nt(qq, k_ref[r0:r0 + ck, :])
            s_ref[t % 2, :, c0:c0 + ck] = s
            for jj in range(ck // HEAD_DIM):
                part = s[:, jj * HEAD_DIM:(jj + 1) * HEAD_DIM]
                mx = part if mx is None else jnp.maximum(mx, part)
        return jnp.max(mx, axis=-1, keepdims=True)

    def softmax_pv_out(t, m):
        acc = None
        for _, v_ref, r0, ck, c0 in chunks:
            p = jnp.exp2(s_ref[t % 2, :, c0:c0 + ck] - m).astype(_BF16)
            v_aug = jnp.concatenate([v_ref[r0:r0 + ck, :], jnp.ones((ck, HEAD_DIM), _BF16)], axis=1)
            pv = jnp.dot(p, v_aug, preferred_element_type=_F32)
            acc = pv if acc is None else acc + pv
        o = acc[:, :HEAD_DIM] / acc[:, HEAD_DIM:]
        oa = o[:tq] - lam * o[tq:]
        rows = slice(t * tq, (t + 1) * tq)
        z = z_ref[rows, :].astype(_F32)
        o_ref[rows, :] = (_rms(oa, g_ref[...]) * (1.0 - lam_init) * _silu(z)).astype(_BF16)

    row_max = scores_and_max(0)
    for t in range(1, n_tiles):
        next_max = scores_and_max(t)
        softmax_pv_out(t - 1, row_max)
        row_max = next_max
    softmax_pv_out(n_tiles - 1, row_max)


def _diff_attn(q_proj, kv_projs, lam_params, da_g, *, lam_init):
    B, Sq, _ = q_proj.shape
    tq = min(DA_TILES_PER_STEP * DA_Q_TILE, Sq)
    in_specs = [
        pl.BlockSpec((None, tq, HEAD_DIM), lambda b, h, i: (b, i, COL_AQ + h)),
        pl.BlockSpec((None, tq, HEAD_DIM), lambda b, h, i: (b, i, COL_AZ + h)),
    ]
    args = [q_proj, q_proj]
    for kp in kv_projs:
        sk = kp.shape[1]
        in_specs.append(pl.BlockSpec((None, sk, HEAD_DIM), lambda b, h, i: (b, 0, COL_AK + h)))
        in_specs.append(pl.BlockSpec((None, sk, HEAD_DIM), lambda b, h, i: (b, 0, COL_AV + h)))
        args += [kp, kp]
    in_specs += [
        pl.BlockSpec((4, DA_QK_DIM), lambda b, h, i: (0, 0)),
        pl.BlockSpec((1, HEAD_DIM), lambda b, h, i: (0, 0)),
    ]
    args += [lam_params, da_g.reshape(1, HEAD_DIM)]
    return pl.pallas_call(
        functools.partial(_da_kernel, n_src=len(kv_projs), lam_init=lam_init),
        grid=(B, DA_HEADS, Sq // tq),
        in_specs=in_specs,
        out_specs=pl.BlockSpec((None, tq, HEAD_DIM), lambda b, h, i: (b, i, h)),
        out_shape=jax.ShapeDtypeStruct((B, Sq, DA_WIDTH), _BF16),
        scratch_shapes=[pltpu.VMEM((2, 2 * DA_Q_TILE, sum(kp.shape[1] for kp in kv_projs)), _F32)],
        name="diff_attn",
        compiler_params=_params("parallel", "parallel", "parallel"),
    )(*args)


def _na_kernel(q_ref, z_ref, k_ref, v_ref, kc_ref, vc_ref, bias_ref, g_ref, o_ref):
    rb = pl.program_id(1)
    rows = k_ref.shape[0] // GRID_W
    kr0 = jnp.clip(rb * NA_BLOCK_ROWS - NA_WIN_H // 2, 0, rows - NA_KEY_ROWS)
    start = pl.multiple_of(kr0 * GRID_W, GRID_W)
    for h in range(NA_HEADS):
        cols = slice(h * HEAD_DIM, (h + 1) * HEAD_DIM)
        k_win = k_ref[pl.ds(start, NA_K), cols]
        v_win = v_ref[pl.ds(start, NA_K), cols]
        o = _softmax_pv(q_ref[:, cols], [(k_win, v_win, bias_ref[h]), (kc_ref[:, cols], vc_ref[:, cols], None)])
        z = z_ref[:, cols].astype(_F32)
        o_ref[:, cols] = (_rms(o, g_ref[:, cols]) * _silu(z)).astype(_BF16)


def _na_bias_kernel(r_ref, o_ref, *, rows):
    shape = (GRID_W, 2 * GRID_W)
    c = lax.broadcasted_iota(jnp.int32, shape, 0)
    lane = lax.broadcasted_iota(jnp.int32, shape, 1)
    kc = lane & (GRID_W - 1)
    cs = jnp.clip(c - NA_WIN_W // 2, 0, GRID_W - NA_WIN_W)
    col_valid = (kc >= cs) & (kc < cs + NA_WIN_W)
    first_key_row = lane < GRID_W
    toeplitz = []
    for t in range(2 * NA_WIN_H):
        row = jnp.broadcast_to(r_ref[t:t + 1, :], shape)
        x = pltpu.roll(row, 2 * GRID_W - (NA_WIN_W - 1), 1, stride=1, stride_axis=0)
        toeplitz.append(jnp.where(col_valid, x * LOG2E, MASK_VALUE))
    masked = jnp.full(shape, MASK_VALUE, _F32)
    for cfg, (r0, kr0) in enumerate(_na_block_configs(rows)):
        for ri in range(NA_BLOCK_ROWS):
            r = r0 + ri
            rs = min(max(r - NA_WIN_H // 2, 0), rows - NA_WIN_H)
            for j in range(NA_KEY_ROWS // 2):
                kra = kr0 + 2 * j
                valid_a = rs <= kra < rs + NA_WIN_H
                valid_b = rs <= kra + 1 < rs + NA_WIN_H
                t = kra - r + NA_WIN_H
                if valid_a and valid_b:
                    blk = toeplitz[t]
                elif valid_a:
                    blk = jnp.where(first_key_row, toeplitz[t], MASK_VALUE)
                elif valid_b:
                    blk = jnp.where(first_key_row, MASK_VALUE, toeplitz[t])
                else:
                    blk = masked
                o_ref[cfg, ri * GRID_W:(ri + 1) * GRID_W, j * 2 * GRID_W:(j + 1) * 2 * GRID_W] = blk


def _na_block_configs(rows):
    return ((0, 0), (2 * NA_BLOCK_ROWS, 2 * NA_BLOCK_ROWS - NA_WIN_H // 2),
            (rows - NA_BLOCK_ROWS, rows - NA_KEY_ROWS))


def _na_bias_tables(rpb, rows):
    H, nr, ncol = rpb.shape
    lo = jnp.pad(rpb, ((0, 0), (1, 0), (0, 2 * GRID_W - ncol)))
    hi = jnp.pad(rpb, ((0, 0), (0, 1), (GRID_W, GRID_W - ncol)))
    return pl.pallas_call(
        functools.partial(_na_bias_kernel, rows=rows),
        grid=(H,),
        in_specs=[pl.BlockSpec((None, nr + 1, 2 * GRID_W), lambda h: (h, 0, 0))],
        out_specs=pl.BlockSpec((3, None, NA_Q, NA_K), lambda h: (0, h, 0, 0)),
        out_shape=jax.ShapeDtypeStruct((3, H, NA_Q, NA_K), _F32),
        name="na_bias",
        compiler_params=_params("parallel"),
    )(lo + hi)


def _nbr_attn(proj, proj_ctx, bias_tabs, na_g):
    B, S, _ = proj.shape
    C = proj_ctx.shape[1]
    n_blocks = S // NA_Q

    def bias_idx(b, i):
        return (jnp.where(i == 0, 0, jnp.where(i == n_blocks - 1, 2, 1)), 0, 0, 0)

    cq, ck, cv, cz = (c * HEAD_DIM // NA_WIDTH for c in (COL_BQ, COL_BK, COL_BV, COL_BZ))
    return pl.pallas_call(
        _na_kernel,
        grid=(B, n_blocks),
        in_specs=[
            pl.BlockSpec((None, NA_Q, NA_WIDTH), lambda b, i: (b, i, cq)),
            pl.BlockSpec((None, NA_Q, NA_WIDTH), lambda b, i: (b, i, cz)),
            pl.BlockSpec((None, S, NA_WIDTH), lambda b, i: (b, 0, ck)),
            pl.BlockSpec((None, S, NA_WIDTH), lambda b, i: (b, 0, cv)),
            pl.BlockSpec((None, C, NA_WIDTH), lambda b, i: (b, 0, ck)),
            pl.BlockSpec((None, C, NA_WIDTH), lambda b, i: (b, 0, cv)),
            pl.BlockSpec((None, NA_HEADS, NA_Q, NA_K), bias_idx),
            pl.BlockSpec((1, NA_WIDTH), lambda b, i: (0, 0)),
        ],
        out_specs=pl.BlockSpec((None, NA_Q, NA_WIDTH), lambda b, i: (b, i, 0)),
        out_shape=jax.ShapeDtypeStruct((B, S, NA_WIDTH), _BF16),
        name="nbr_attn",
        compiler_params=_params("parallel", "parallel"),
    )(proj, proj, proj, proj, proj_ctx, proj_ctx, bias_tabs, na_g.reshape(1, NA_WIDTH))


def _ctx_attn_kernel(q_ref, z_ref, k_ref, v_ref, g_ref, o_ref):
    o = _softmax_pv(q_ref[...], [(k_ref[...], v_ref[...], None)])
    _branch_out(o, g_ref[...], 1.0, z_ref, o_ref)


def _ctx_attn(proj_ctx, na_g):
    B, C, _ = proj_ctx.shape
    return pl.pallas_call(
        _ctx_attn_kernel,
        grid=(B, NA_HEADS),
        in_specs=[
            pl.BlockSpec((None, C, HEAD_DIM), lambda b, h: (b, 0, COL_BQ + h)),
            pl.BlockSpec((None, C, HEAD_DIM), lambda b, h: (b, 0, COL_BZ + h)),
            pl.BlockSpec((None, C, HEAD_DIM), lambda b, h: (b, 0, COL_BK + h)),
            pl.BlockSpec((None, C, HEAD_DIM), lambda b, h: (b, 0, COL_BV + h)),
            pl.BlockSpec((None, 1, HEAD_DIM), lambda b, h: (h, 0, 0)),
        ],
        out_specs=pl.BlockSpec((None, C, HEAD_DIM), lambda b, h: (b, 0, h)),
        out_shape=jax.ShapeDtypeStruct((B, C, NA_WIDTH), _BF16),
        name="ctx_attn",
        compiler_params=_params("parallel", "parallel"),
    )(proj_ctx, proj_ctx, proj_ctx, proj_ctx, na_g.reshape(NA_HEADS, 1, HEAD_DIM))


def _mlp_kernel(u_ref, v_ref, z_ref, lng_ref, lnb_ref, ws_ref, bs_ref, g_ref, o_ref):
    n_chunks = u_ref.shape[0] // CHUNK
    for g in range(MLP_GROUPS):
        cols = slice(g * HEAD_DIM, (g + 1) * HEAD_DIM)
        w = ws_ref[g].astype(_BF16)
        bs = jnp.concatenate([bs_ref[g]] * n_chunks, axis=0)
        v = v_ref[:, cols].astype(_F32)
        vc = v - jnp.mean(v, axis=-1, keepdims=True)
        vn = vc * lax.rsqrt(jnp.mean(vc * vc, axis=-1, keepdims=True) + EPS) * lng_ref[:, cols] + lnb_ref[:, cols]
        vn = vn.astype(_BF16)
        mixed = jnp.concatenate(
            [jnp.dot(w, vn[c * CHUNK:(c + 1) * CHUNK, :], preferred_element_type=_F32) for c in range(n_chunks)],
            axis=0) + bs
        om = u_ref[:, cols].astype(_F32) * mixed
        z = z_ref[:, cols].astype(_F32)
        o_ref[:, cols] = (_rms(om, g_ref[:, cols]) * _silu(z)).astype(_BF16)


def _gmlp(proj, ln_g, ln_b, ws, bs, out_g, *, rows_per_step):
    B, S, _ = proj.shape
    return pl.pallas_call(
        _mlp_kernel,
        grid=(B, S // rows_per_step),
        in_specs=[
            pl.BlockSpec((None, rows_per_step, MLP_WIDTH), lambda b, i: (b, i, COL_CU)),
            pl.BlockSpec((None, rows_per_step, MLP_WIDTH), lambda b, i: (b, i, COL_CV)),
            pl.BlockSpec((None, rows_per_step, MLP_WIDTH), lambda b, i: (b, i, COL_CZ)),
            pl.BlockSpec((1, MLP_WIDTH), lambda b, i: (0, 0)),
            pl.BlockSpec((1, MLP_WIDTH), lambda b, i: (0, 0)),
            pl.BlockSpec((MLP_GROUPS, CHUNK, CHUNK), lambda b, i: (0, 0, 0)),
            pl.BlockSpec((MLP_GROUPS, CHUNK, 1), lambda b, i: (0, 0, 0)),
            pl.BlockSpec((1, MLP_WIDTH), lambda b, i: (0, 0)),
        ],
        out_specs=pl.BlockSpec((None, rows_per_step, MLP_WIDTH), lambda b, i: (b, i, 0)),
        out_shape=jax.ShapeDtypeStruct((B, S, MLP_WIDTH), _BF16),
        name="gmlp",
        compiler_params=_params("parallel", "parallel"),
    )(proj, proj, proj, ln_g.reshape(1, MLP_WIDTH), ln_b.reshape(1, MLP_WIDTH), ws,
      bs.reshape(MLP_GROUPS, CHUNK, 1), out_g.reshape(1, MLP_WIDTH))


def _out_kernel(ya_ref, yn_ref, ym_ref, w_ref, x_ref, gate_ref, g_ref, o_ref):
    for r0 in range(0, x_ref.shape[0], OUT_SUB_ROWS):
        rows = slice(r0, r0 + OUT_SUB_ROWS)
        y = jnp.dot(ya_ref[rows, :], w_ref[:DA_WIDTH, :], preferred_element_type=_F32)
        y = y + jnp.dot(yn_ref[rows, :], w_ref[DA_WIDTH:DA_WIDTH + NA_WIDTH, :], preferred_element_type=_F32)
        y = y + jnp.dot(ym_ref[rows, :], w_ref[DA_WIDTH + NA_WIDTH:, :], preferred_element_type=_F32)
        o_ref[rows, :] = x_ref[rows, :] + gate_ref[...] * _rms(y, g_ref[...])


def _out_proj(ya, yn, ym, w_bf16, layer, x2d, gate, post_g):
    T, D = x2d.shape
    nb = gate.shape[0]
    tm = min(OUT_TILE_ROWS, T // nb)
    tiles_per_mod = (T // nb) // tm
    return pl.pallas_call(
        _out_kernel,
        grid=(T // tm,),
        in_specs=[
            pl.BlockSpec((tm, DA_WIDTH), lambda i: (i, 0)),
            pl.BlockSpec((tm, NA_WIDTH), lambda i: (i, 0)),
            pl.BlockSpec((tm, MLP_WIDTH), lambda i: (i, 0)),
            pl.BlockSpec((None, MIX_WIDTH, D), lambda i: (layer, 0, 0), pipeline_mode=pl.Buffered(1)),
            pl.BlockSpec((tm, D), lambda i: (i, 0)),
            pl.BlockSpec((None, 1, D), lambda i: (i // tiles_per_mod, 0, 0)),
            pl.BlockSpec((1, D), lambda i: (0, 0)),
        ],
        out_specs=pl.BlockSpec((tm, D), lambda i: (i, 0)),
        out_shape=jax.ShapeDtypeStruct((T, D), _F32),
        name="out_proj",
        compiler_params=_params("parallel"),
    )(ya.reshape(T, DA_WIDTH), yn.reshape(T, NA_WIDTH), ym.reshape(T, MLP_WIDTH), w_bf16, x2d,
      gate, post_g.reshape(1, D))


def _rope_tables(seq):
    n = DA_QK_DIM // 4
    freqs = ROPE_BASE ** (-np.arange(n, dtype=np.float64) / n)
    pos = np.arange(seq)
    row = (pos // GRID_W).astype(np.float64)[:, None] * freqs
    col = (pos % GRID_W).astype(np.float64)[:, None] * freqs
    zeros = np.zeros((seq, n))

    def one_map(fn_first, fn_second):
        return np.concatenate([fn_first(row), fn_second(row), fn_first(col), fn_second(col)], axis=-1)

    cos = one_map(np.cos, np.cos)
    sa = one_map(lambda a: -np.sin(a), lambda a: zeros)
    sb = one_map(lambda a: zeros, np.sin)
    return tuple(jnp.asarray(np.concatenate([t, t], axis=-1), _F32) for t in (cos, sa, sb))


def _identity_rope_tables(seq):
    return (jnp.ones((seq, HEAD_DIM), _F32), jnp.zeros((seq, HEAD_DIM), _F32),
            jnp.zeros((seq, HEAD_DIM), _F32))


def kernel(x, c, ctx, c_ctx, ada_w, ada_b, pre_g, post_g, w_in, w_out, lam_q1, lam_k1, lam_q2, lam_k2,
           da_g, na_rpb, na_g, mlp_ln_g, mlp_ln_b, mlp_ws, mlp_bs, mlp_g):
    B, S, D = x.shape
    C = ctx.shape[1]
    L = ada_w.shape[0]
    assert S % NA_Q == 0 and S // GRID_W >= NA_KEY_ROWS and C % CHUNK == 0

    cc = jnp.concatenate([c, c_ctx[None], jnp.zeros((8 - B - 1, D), _F32)], axis=0)
    mod = _ada(cc, ada_w, ada_b)

    rope = _rope_tables(S)
    rope_id = _identity_rope_tables(C)
    w_in_b = w_in.astype(_BF16)
    w_out_b = w_out.astype(_BF16)
    x2d = x.reshape(B * S, D)
    xc2d = ctx.reshape(B * C, D)

    for l in range(L):
        lam_init = 0.8 - 0.6 * math.exp(-0.3 * l)
        ctx_out = l < L - 1
        shift, scale, gate = (mod[l, :B, i * D:(i + 1) * D].reshape(B, 1, D) for i in range(3))
        shift_c, scale_c, gate_c = (mod[l, B:B + 1, i * D:(i + 1) * D].reshape(1, 1, D) for i in range(3))
        lam_params = jnp.stack([lam_q1[l], lam_k1[l], lam_q2[l], lam_k2[l]])

        proj = _in_proj(x2d, pre_g[l], scale, shift, w_in_b, l, *rope, seq=S).reshape(B, S, IN_WIDTH)
        proj_c = _in_proj(xc2d, pre_g[l], scale_c, shift_c, w_in_b, l, *rope_id, seq=C).reshape(B, C, IN_WIDTH)

        ya = _diff_attn(proj, [proj, proj_c], lam_params, da_g[l], lam_init=lam_init)
        yn = _nbr_attn(proj, proj_c, _na_bias_tables(na_rpb[l], S // GRID_W), na_g[l])
        ym = _gmlp(proj, mlp_ln_g[l], mlp_ln_b[l], mlp_ws[l], mlp_bs[l], mlp_g[l], rows_per_step=512)
        x2d_new = _out_proj(ya, yn, ym, w_out_b, l, x2d, gate, post_g[l])

        if ctx_out:
            yca = _diff_attn(proj_c, [proj_c], lam_params, da_g[l], lam_init=lam_init)
            ycn = _ctx_attn(proj_c, na_g[l])
            ycm = _gmlp(proj_c, mlp_ln_g[l], mlp_ln_b[l], mlp_ws[l], mlp_bs[l], mlp_g[l], rows_per_step=C)
            xc2d = _out_proj(yca, ycn, ycm, w_out_b, l, xc2d, gate_c, post_g[l])
        x2d = x2d_new

    return x2d.reshape(B, S, D)
```

```python
import functools
import math

import numpy as np
import jax
import jax.numpy as jnp
from jax import lax
from jax.experimental import pallas as pl
from jax.experimental.pallas import tpu as pltpu

D_MODEL = 2048
DEPTH = 2
GRID_W = 64
HEAD_DIM = 128
DA_HEADS = 6
NA_HEADS = 6
MLP_GROUPS = 4
DA_WIDTH = DA_HEADS * HEAD_DIM
NA_WIDTH = NA_HEADS * HEAD_DIM
MLP_WIDTH = MLP_GROUPS * HEAD_DIM
DA_QK_DIM = HEAD_DIM // 2
NA_WIN_H = 8
NA_WIN_W = 16
CHUNK = 128
ROPE_BASE = 10000.0
EPS = 1e-6
IN_WIDTH = 4 * DA_WIDTH + 4 * NA_WIDTH + 3 * MLP_WIDTH
MIX_WIDTH = DA_WIDTH + NA_WIDTH + MLP_WIDTH

COL_AQ, COL_AK, COL_AV, COL_AZ = 0, 6, 12, 18
COL_BQ, COL_BK, COL_BV, COL_BZ = 24, 30, 36, 42
COL_CU, COL_CV, COL_CZ = 12, 13, 14

LOG2E = math.log2(math.e)
DA_SCALE = DA_QK_DIM ** -0.5 * LOG2E
NA_SCALE = HEAD_DIM ** -0.5 * LOG2E
DA_KEY_CHUNK = 512
DA_Q_TILE = 128
DA_TILES_PER_STEP = 8


def _in_segments():
    kinds = ["rope_scale", "rope", "plain", "plain", "na_scale", "plain", "plain", "plain", "gelu", "gelu", "plain"]
    widths = [DA_WIDTH] * 4 + [NA_WIDTH] * 4 + [MLP_WIDTH] * 3
    segs, c0 = [], 0
    for width, kind in zip(widths, kinds):
        segs.append((c0, width, kind))
        c0 += width
    return tuple(segs)


IN_SEGMENTS = _in_segments()
IN_TILE_ROWS = 256
OUT_TILE_ROWS = 512
OUT_SUB_ROWS = 256

NA_BLOCK_ROWS = 4
NA_KEY_ROWS = NA_BLOCK_ROWS + NA_WIN_H
NA_BLOCKS_PER_STEP = 2
NA_Q = NA_BLOCK_ROWS * GRID_W
NA_K = NA_KEY_ROWS * GRID_W
MASK_VALUE = -1e30

VMEM_LIMIT_BYTES = 56 * 1024 * 1024

_BF16 = jnp.bfloat16
_F32 = jnp.float32


def _silu(x):
    return x * (1.0 / (1.0 + jnp.exp(-x)))


def _gelu_tanh(x):
    c = math.sqrt(2.0 / math.pi)
    return 0.5 * x * (1.0 + jnp.tanh(c * (x + 0.044715 * (x * x * x))))


def _rms(x, g):
    return x * lax.rsqrt(jnp.mean(x * x, axis=-1, keepdims=True) + EPS) * g


def _dot_nt(a, b):
    return lax.dot_general(a, b, (((1,), (1,)), ((), ())), preferred_element_type=_F32)


def _params(*sem):
    return pltpu.CompilerParams(dimension_semantics=sem, vmem_limit_bytes=VMEM_LIMIT_BYTES)


def _ada_kernel(c_ref, w_ref, b_ref, o_ref):
    sc = _silu(c_ref[...]).astype(_BF16)
    o_ref[...] = jnp.dot(sc, w_ref[...].astype(_BF16), preferred_element_type=_F32) + b_ref[...]


def _ada(cc, ada_w, ada_b):
    L, D, N = ada_w.shape
    R = cc.shape[0]
    tn = 768
    return pl.pallas_call(
        _ada_kernel,
        grid=(L, N // tn),
        in_specs=[
            pl.BlockSpec((R, D), lambda l, j: (0, 0)),
            pl.BlockSpec((None, D, tn), lambda l, j: (l, 0, j)),
            pl.BlockSpec((None, 1, tn), lambda l, j: (l, 0, j)),
        ],
        out_specs=pl.BlockSpec((None, R, tn), lambda l, j: (l, 0, j)),
        out_shape=jax.ShapeDtypeStruct((L, R, N), _F32),
        name="ada_mod",
        compiler_params=_params("parallel", "parallel"),
    )(cc, ada_w, ada_b.reshape(L, 1, N))


def _in_kernel(x_ref, g_ref, scale_ref, shift_ref, w_ref, cos_ref, sa_ref, sb_ref, o_ref):
    x = x_ref[...]
    h = (_rms(x, g_ref[...]) * (1.0 + scale_ref[...]) + shift_ref[...]).astype(_BF16)
    cos, sa, sb = cos_ref[...], sa_ref[...], sb_ref[...]
    for c0, width, epilogue in IN_SEGMENTS:
        acc = jnp.dot(h, w_ref[:, c0:c0 + width], preferred_element_type=_F32)
        if epilogue in ("rope", "rope_scale"):
            for hh in range(width // HEAD_DIM):
                xs = acc[:, hh * HEAD_DIM:(hh + 1) * HEAD_DIM]
                r = xs * cos + pltpu.roll(xs, HEAD_DIM - 16, 1) * sa + pltpu.roll(xs, 16, 1) * sb
                if epilogue == "rope_scale":
                    r = r * DA_SCALE
                o_ref[:, c0 + hh * HEAD_DIM:c0 + (hh + 1) * HEAD_DIM] = r.astype(_BF16)
        elif epilogue == "na_scale":
            o_ref[:, c0:c0 + width] = (acc * NA_SCALE).astype(_BF16)
        elif epilogue == "gelu":
            o_ref[:, c0:c0 + width] = _gelu_tanh(acc).astype(_BF16)
        else:
            o_ref[:, c0:c0 + width] = acc.astype(_BF16)


def _in_proj(x2d, pre_g, scale, shift, w_bf16, layer, cos, sa, sb, *, seq):
    T, D = x2d.shape
    nb = scale.shape[0]
    tm = IN_TILE_ROWS
    tiles_per_mod = (T // nb) // tm
    tiles_per_seq = seq // tm
    return pl.pallas_call(
        _in_kernel,
        grid=(T // tm,),
        in_specs=[
            pl.BlockSpec((tm, D), lambda i: (i, 0)),
            pl.BlockSpec((1, D), lambda i: (0, 0)),
            pl.BlockSpec((None, 1, D), lambda i: (i // tiles_per_mod, 0, 0)),
            pl.BlockSpec((None, 1, D), lambda i: (i // tiles_per_mod, 0, 0)),
            pl.BlockSpec((None, D, IN_WIDTH), lambda i: (layer, 0, 0), pipeline_mode=pl.Buffered(1)),
            pl.BlockSpec((tm, HEAD_DIM), lambda i: (i % tiles_per_seq, 0)),
            pl.BlockSpec((tm, HEAD_DIM), lambda i: (i % tiles_per_seq, 0)),
            pl.BlockSpec((tm, HEAD_DIM), lambda i: (i % tiles_per_seq, 0)),
        ],
        out_specs=pl.BlockSpec((tm, IN_WIDTH), lambda i: (i, 0)),
        out_shape=jax.ShapeDtypeStruct((T, IN_WIDTH), _BF16),
        name="in_proj",
        compiler_params=_params("parallel"),
    )(x2d, pre_g.reshape(1, D), scale, shift, w_bf16, cos, sa, sb)


def _softmax_pv(qq, sources):
    scores = []
    for k, _, bias in sources:
        s = _dot_nt(qq, k)
        if bias is not None:
            s = s + bias
        scores.append(s)
    m = jnp.max(scores[0], axis=-1, keepdims=True)
    for s in scores[1:]:
        m = jnp.maximum(m, jnp.max(s, axis=-1, keepdims=True))
    acc = None
    for s, (_, v, _) in zip(scores, sources):
        p = jnp.exp2(s - m).astype(_BF16)
        v_aug = jnp.concatenate([v, jnp.ones(v.shape, _BF16)], axis=1)
        pv = jnp.dot(p, v_aug, preferred_element_type=_F32)
        acc = pv if acc is None else acc + pv
    return acc[:, :HEAD_DIM] / acc[:, HEAD_DIM:]


def _branch_out(o, g, factor, z_ref, o_ref):
    z = z_ref[...].astype(_F32)
    o_ref[...] = (_rms(o, g) * factor * _silu(z)).astype(_BF16)


def _da_kernel(*refs, n_src, lam_init):
    q_ref, z_ref = refs[0], refs[1]
    kv_refs = refs[2:2 + 2 * n_src]
    lam_ref, g_ref, o_ref, s_ref = refs[2 + 2 * n_src:]
    tq = DA_Q_TILE
    n_tiles = q_ref.shape[0] // tq

    chunks = []
    col = 0
    for i in range(n_src):
        k_ref, v_ref = kv_refs[2 * i], kv_refs[2 * i + 1]
        size = k_ref.shape[0]
        ck = min(size, DA_KEY_CHUNK)
        for c in range(size // ck):
            chunks.append((k_ref, v_ref, c * ck, ck, col))
            col += ck

    lp = lam_ref[...]
    lam = (jnp.exp(jnp.sum(lp[0:1] * lp[1:2], axis=-1, keepdims=True))
           - jnp.exp(jnp.sum(lp[2:3] * lp[3:4], axis=-1, keepdims=True)) + lam_init)

    def scores_and_max(t):
        q = q_ref[t * tq:(t + 1) * tq, :]
        lane = lax.broadcasted_iota(jnp.int32, q.shape, 1)
        zero = jnp.zeros_like(q)
        qq = jnp.concatenate([jnp.where(lane < DA_QK_DIM, q, zero),
                              jnp.where(lane >= DA_QK_DIM, q, zero)], axis=0)
        mx = None
        for k_ref, _, r0, ck, c0 in chunks:
            s = _dot_nt(qq, k_ref[r0:r0 + ck, :])
            s_ref[t % 2, :, c0:c0 + ck] = s
            for jj in range(ck // HEAD_DIM):
                part = s[:, jj * HEAD_DIM:(jj + 1) * HEAD_DIM]
                mx = part if mx is None else jnp.maximum(mx, part)
        return jnp.max(mx, axis=-1, keepdims=True)

    def softmax_pv_out(t, m):
        acc = None
        for _, v_ref, r0, ck, c0 in chunks:
            p = jnp.exp2(s_ref[t % 2, :, c0:c0 + ck] - m).astype(_BF16)
            v_aug = jnp.concatenate([v_ref[r0:r0 + ck, :], jnp.ones((ck, HEAD_DIM), _BF16)], axis=1)
            pv = jnp.dot(p, v_aug, preferred_element_type=_F32)
            acc = pv if acc is None else acc + pv
        o = acc[:, :HEAD_DIM] / acc[:, HEAD_DIM:]
        oa = o[:tq] - lam * o[tq:]
        rows = slice(t * tq, (t + 1) * tq)
        z = z_ref[rows, :].astype(_F32)
        o_ref[rows, :] = (_rms(oa, g_ref[...]) * (1.0 - lam_init) * _silu(z)).astype(_BF16)

    row_max = scores_and_max(0)
    for t in range(1, n_tiles):
        next_max = scores_and_max(t)
        softmax_pv_out(t - 1, row_max)
        row_max = next_max
    softmax_pv_out(n_tiles - 1, row_max)


def _diff_attn(q_proj, kv_projs, lam_params, da_g, *, lam_init):
    B, Sq, _ = q_proj.shape
    tq = min(DA_TILES_PER_STEP * DA_Q_TILE, Sq)
    in_specs = [
        pl.BlockSpec((None, tq, HEAD_DIM), lambda b, h, i: (b, i, COL_AQ + h)),
        pl.BlockSpec((None, tq, HEAD_DIM), lambda b, h, i: (b, i, COL_AZ + h)),
    ]
    args = [q_proj, q_proj]
    for kp in kv_projs:
        sk = kp.shape[1]
        in_specs.append(pl.BlockSpec((None, sk, HEAD_DIM), lambda b, h, i: (b, 0, COL_AK + h)))
        in_specs.append(pl.BlockSpec((None, sk, HEAD_DIM), lambda b, h, i: (b, 0, COL_AV + h)))
        args += [kp, kp]
    in_specs += [
        pl.BlockSpec((4, DA_QK_DIM), lambda b, h, i: (0, 0)),
        pl.BlockSpec((1, HEAD_DIM), lambda b, h, i: (0, 0)),
    ]
    args += [lam_params, da_g.reshape(1, HEAD_DIM)]
    return pl.pallas_call(
        functools.partial(_da_kernel, n_src=len(kv_projs), lam_init=lam_init),
        grid=(B, DA_HEADS, Sq // tq),
        in_specs=in_specs,
        out_specs=pl.BlockSpec((None, tq, HEAD_DIM), lambda b, h, i: (b, i, h)),
        out_shape=jax.ShapeDtypeStruct((B, Sq, DA_WIDTH), _BF16),
        scratch_shapes=[pltpu.VMEM((2, 2 * DA_Q_TILE, sum(kp.shape[1] for kp in kv_projs)), _F32)],
        name="diff_attn",
        compiler_params=_params("parallel", "parallel", "parallel"),
    )(*args)


def _na_kernel(q_ref, z_ref, k_ref, v_ref, kc_ref, vc_ref, *rest):
    bias_refs, (g_ref, o_ref) = rest[:NA_BLOCKS_PER_STEP], rest[NA_BLOCKS_PER_STEP:]
    step = pl.program_id(1)
    rows = k_ref.shape[0] // GRID_W
    n_ctx = kc_ref.shape[0]
    ones = jnp.ones((NA_K, HEAD_DIM), _BF16)

    def key_start(j):
        rb = step * NA_BLOCKS_PER_STEP + j
        kr0 = jnp.clip(rb * NA_BLOCK_ROWS - NA_WIN_H // 2, 0, rows - NA_KEY_ROWS)
        return pl.multiple_of(kr0 * GRID_W, GRID_W)

    def scores_and_max(j, h):
        cols = slice(h * HEAD_DIM, (h + 1) * HEAD_DIM)
        q = q_ref[j * NA_Q:(j + 1) * NA_Q, cols]
        s_loc = _dot_nt(q, k_ref[pl.ds(key_start(j), NA_K), cols]) + bias_refs[j][h]
        s_ctx = _dot_nt(q, kc_ref[:, cols])
        m = jnp.maximum(jnp.max(s_loc, axis=-1, keepdims=True), jnp.max(s_ctx, axis=-1, keepdims=True))
        return s_loc, s_ctx, m

    def softmax_pv_out(j, h, s_loc, s_ctx, m):
        cols = slice(h * HEAD_DIM, (h + 1) * HEAD_DIM)
        v_loc = jnp.concatenate([v_ref[pl.ds(key_start(j), NA_K), cols], ones], axis=1)
        v_ctx = jnp.concatenate([vc_ref[:, cols], ones[:n_ctx]], axis=1)
        acc = jnp.dot(jnp.exp2(s_loc - m).astype(_BF16), v_loc, preferred_element_type=_F32)
        acc = acc + jnp.dot(jnp.exp2(s_ctx - m).astype(_BF16), v_ctx, preferred_element_type=_F32)
        o = acc[:, :HEAD_DIM] / acc[:, HEAD_DIM:]
        qrows = slice(j * NA_Q, (j + 1) * NA_Q)
        z = z_ref[qrows, cols].astype(_F32)
        o_ref[qrows, cols] = (_rms(o, g_ref[:, cols]) * _silu(z)).astype(_BF16)

    units = [(j, h) for j in range(NA_BLOCKS_PER_STEP) for h in range(NA_HEADS)]
    pending = scores_and_max(*units[0])
    for u in range(1, len(units)):
        nxt = scores_and_max(*units[u])
        softmax_pv_out(*units[u - 1], *pending)
        pending = nxt
    softmax_pv_out(*units[-1], *pending)


def _na_bias_kernel(r_ref, o_ref, *, rows):
    shape = (GRID_W, 2 * GRID_W)
    c = lax.broadcasted_iota(jnp.int32, shape, 0)
    lane = lax.broadcasted_iota(jnp.int32, shape, 1)
    kc = lane & (GRID_W - 1)
    cs = jnp.clip(c - NA_WIN_W // 2, 0, GRID_W - NA_WIN_W)
    col_valid = (kc >= cs) & (kc < cs + NA_WIN_W)
    first_key_row = lane < GRID_W
    toeplitz = []
    for t in range(2 * NA_WIN_H):
        row = jnp.broadcast_to(r_ref[t:t + 1, :], shape)
        x = pltpu.roll(row, 2 * GRID_W - (NA_WIN_W - 1), 1, stride=1, stride_axis=0)
        toeplitz.append(jnp.where(col_valid, x * LOG2E, MASK_VALUE))
    masked = jnp.full(shape, MASK_VALUE, _F32)
    for cfg, (r0, kr0) in enumerate(_na_block_configs(rows)):
        for ri in range(NA_BLOCK_ROWS):
            r = r0 + ri
            rs = min(max(r - NA_WIN_H // 2, 0), rows - NA_WIN_H)
            for j in range(NA_KEY_ROWS // 2):
                kra = kr0 + 2 * j
                valid_a = rs <= kra < rs + NA_WIN_H
                valid_b = rs <= kra + 1 < rs + NA_WIN_H
                t = kra - r + NA_WIN_H
                if valid_a and valid_b:
                    blk = toeplitz[t]
                elif valid_a:
                    blk = jnp.where(first_key_row, toeplitz[t], MASK_VALUE)
                elif valid_b:
                    blk = jnp.where(first_key_row, MASK_VALUE, toeplitz[t])
                else:
                    blk = masked
                o_ref[cfg, ri * GRID_W:(ri + 1) * GRID_W, j * 2 * GRID_W:(j + 1) * 2 * GRID_W] = blk


def _na_block_configs(rows):
    return ((0, 0), (2 * NA_BLOCK_ROWS, 2 * NA_BLOCK_ROWS - NA_WIN_H // 2),
            (rows - NA_BLOCK_ROWS, rows - NA_KEY_ROWS))


def _na_bias_tables(rpb, rows):
    H, nr, ncol = rpb.shape
    lo = jnp.pad(rpb, ((0, 0), (1, 0), (0, 2 * GRID_W - ncol)))
    hi = jnp.pad(rpb, ((0, 0), (0, 1), (GRID_W, GRID_W - ncol)))
    return pl.pallas_call(
        functools.partial(_na_bias_kernel, rows=rows),
        grid=(H,),
        in_specs=[pl.BlockSpec((None, nr + 1, 2 * GRID_W), lambda h: (h, 0, 0))],
        out_specs=pl.BlockSpec((3, None, NA_Q, NA_K), lambda h: (0, h, 0, 0)),
        out_shape=jax.ShapeDtypeStruct((3, H, NA_Q, NA_K), _F32),
        name="na_bias",
        compiler_params=_params("parallel"),
    )(lo + hi)


def _nbr_attn(proj, proj_ctx, bias_tabs, na_g):
    B, S, _ = proj.shape
    C = proj_ctx.shape[1]
    n_blocks = S // NA_Q
    step_q = NA_BLOCKS_PER_STEP * NA_Q

    def bias_spec(j):
        def idx(b, i):
            blk = i * NA_BLOCKS_PER_STEP + j
            return (jnp.where(blk == 0, 0, jnp.where(blk == n_blocks - 1, 2, 1)), 0, 0, 0)
        return pl.BlockSpec((None, NA_HEADS, NA_Q, NA_K), idx)

    cq, ck, cv, cz = (c * HEAD_DIM // NA_WIDTH for c in (COL_BQ, COL_BK, COL_BV, COL_BZ))
    return pl.pallas_call(
        _na_kernel,
        grid=(B, n_blocks // NA_BLOCKS_PER_STEP),
        in_specs=[
            pl.BlockSpec((None, step_q, NA_WIDTH), lambda b, i: (b, i, cq)),
            pl.BlockSpec((None, step_q, NA_WIDTH), lambda b, i: (b, i, cz)),
            pl.BlockSpec((None, S, NA_WIDTH), lambda b, i: (b, 0, ck), pipeline_mode=pl.Buffered(1)),
            pl.BlockSpec((None, S, NA_WIDTH), lambda b, i: (b, 0, cv), pipeline_mode=pl.Buffered(1)),
            pl.BlockSpec((None, C, NA_WIDTH), lambda b, i: (b, 0, ck)),
            pl.BlockSpec((None, C, NA_WIDTH), lambda b, i: (b, 0, cv)),
            *[bias_spec(j) for j in range(NA_BLOCKS_PER_STEP)],
            pl.BlockSpec((1, NA_WIDTH), lambda b, i: (0, 0)),
        ],
        out_specs=pl.BlockSpec((None, step_q, NA_WIDTH), lambda b, i: (b, i, 0)),
        out_shape=jax.ShapeDtypeStruct((B, S, NA_WIDTH), _BF16),
        name="nbr_attn",
        compiler_params=_params("parallel", "parallel"),
    )(proj, proj, proj, proj, proj_ctx, proj_ctx, *([bias_tabs] * NA_BLOCKS_PER_STEP), na_g.reshape(1, NA_WIDTH))


def _ctx_attn_kernel(q_ref, z_ref, k_ref, v_ref, g_ref, o_ref):
    o = _softmax_pv(q_ref[...], [(k_ref[...], v_ref[...], None)])
    _branch_out(o, g_ref[...], 1.0, z_ref, o_ref)


def _ctx_attn(proj_ctx, na_g):
    B, C, _ = proj_ctx.shape
    return pl.pallas_call(
        _ctx_attn_kernel,
        grid=(B, NA_HEADS),
        in_specs=[
            pl.BlockSpec((None, C, HEAD_DIM), lambda b, h: (b, 0, COL_BQ + h)),
            pl.BlockSpec((None, C, HEAD_DIM), lambda b, h: (b, 0, COL_BZ + h)),
            pl.BlockSpec((None, C, HEAD_DIM), lambda b, h: (b, 0, COL_BK + h)),
            pl.BlockSpec((None, C, HEAD_DIM), lambda b, h: (b, 0, COL_BV + h)),
            pl.BlockSpec((None, 1, HEAD_DIM), lambda b, h: (h, 0, 0)),
        ],
        out_specs=pl.BlockSpec((None, C, HEAD_DIM), lambda b, h: (b, 0, h)),
        out_shape=jax.ShapeDtypeStruct((B, C, NA_WIDTH), _BF16),
        name="ctx_attn",
        compiler_params=_params("parallel", "parallel"),
    )(proj_ctx, proj_ctx, proj_ctx, proj_ctx, na_g.reshape(NA_HEADS, 1, HEAD_DIM))


def _mlp_kernel(u_ref, v_ref, z_ref, lng_ref, lnb_ref, ws_ref, bs_ref, g_ref, o_ref):
    n_chunks = u_ref.shape[0] // CHUNK
    for g in range(MLP_GROUPS):
        cols = slice(g * HEAD_DIM, (g + 1) * HEAD_DIM)
        w = ws_ref[g].astype(_BF16)
        bs = jnp.concatenate([bs_ref[g]] * n_chunks, axis=0)
        v = v_ref[:, cols].astype(_F32)
        vc = v - jnp.mean(v, axis=-1, keepdims=True)
        vn = vc * lax.rsqrt(jnp.mean(vc * vc, axis=-1, keepdims=True) + EPS) * lng_ref[:, cols] + lnb_ref[:, cols]
        vn = vn.astype(_BF16)
        mixed = jnp.concatenate(
            [jnp.dot(w, vn[c * CHUNK:(c + 1) * CHUNK, :], preferred_element_type=_F32) for c in range(n_chunks)],
            axis=0) + bs
        om = u_ref[:, cols].astype(_F32) * mixed
        z = z_ref[:, cols].astype(_F32)
        o_ref[:, cols] = (_rms(om, g_ref[:, cols]) * _silu(z)).astype(_BF16)


def _gmlp(proj, ln_g, ln_b, ws, bs, out_g, *, rows_per_step):
    B, S, _ = proj.shape
    return pl.pallas_call(
        _mlp_kernel,
        grid=(B, S // rows_per_step),
        in_specs=[
            pl.BlockSpec((None, rows_per_step, MLP_WIDTH), lambda b, i: (b, i, COL_CU)),
            pl.BlockSpec((None, rows_per_step, MLP_WIDTH), lambda b, i: (b, i, COL_CV)),
            pl.BlockSpec((None, rows_per_step, MLP_WIDTH), lambda b, i: (b, i, COL_CZ)),
            pl.BlockSpec((1, MLP_WIDTH), lambda b, i: (0, 0)),
            pl.BlockSpec((1, MLP_WIDTH), lambda b, i: (0, 0)),
            pl.BlockSpec((MLP_GROUPS, CHUNK, CHUNK), lambda b, i: (0, 0, 0)),
            pl.BlockSpec((MLP_GROUPS, CHUNK, 1), lambda b, i: (0, 0, 0)),
            pl.BlockSpec((1, MLP_WIDTH), lambda b, i: (0, 0)),
        ],
        out_specs=pl.BlockSpec((None, rows_per_step, MLP_WIDTH), lambda b, i: (b, i, 0)),
        out_shape=jax.ShapeDtypeStruct((B, S, MLP_WIDTH), _BF16),
        name="gmlp",
        compiler_params=_params("parallel", "parallel"),
    )(proj, proj, proj, ln_g.reshape(1, MLP_WIDTH), ln_b.reshape(1, MLP_WIDTH), ws,
      bs.reshape(MLP_GROUPS, CHUNK, 1), out_g.reshape(1, MLP_WIDTH))


def _out_kernel(ya_ref, yn_ref, ym_ref, w_ref, x_ref, gate_ref, g_ref, o_ref):
    for r0 in range(0, x_ref.shape[0], OUT_SUB_ROWS):
        rows = slice(r0, r0 + OUT_SUB_ROWS)
        y = jnp.dot(ya_ref[rows, :], w_ref[:DA_WIDTH, :], preferred_element_type=_F32)
        y = y + jnp.dot(yn_ref[rows, :], w_ref[DA_WIDTH:DA_WIDTH + NA_WIDTH, :], preferred_element_type=_F32)
        y = y + jnp.dot(ym_ref[rows, :], w_ref[DA_WIDTH + NA_WIDTH:, :], preferred_element_type=_F32)
        o_ref[rows, :] = x_ref[rows, :] + gate_ref[...] * _rms(y, g_ref[...])


def _out_proj(ya, yn, ym, w_bf16, layer, x2d, gate, post_g):
    T, D = x2d.shape
    nb = gate.shape[0]
    tm = min(OUT_TILE_ROWS, T // nb)
    tiles_per_mod = (T // nb) // tm
    return pl.pallas_call(
        _out_kernel,
        grid=(T // tm,),
        in_specs=[
            pl.BlockSpec((tm, DA_WIDTH), lambda i: (i, 0)),
            pl.BlockSpec((tm, NA_WIDTH), lambda i: (i, 0)),
            pl.BlockSpec((tm, MLP_WIDTH), lambda i: (i, 0)),
            pl.BlockSpec((None, MIX_WIDTH, D), lambda i: (layer, 0, 0), pipeline_mode=pl.Buffered(1)),
            pl.BlockSpec((tm, D), lambda i: (i, 0)),
            pl.BlockSpec((None, 1, D), lambda i: (i // tiles_per_mod, 0, 0)),
            pl.BlockSpec((1, D), lambda i: (0, 0)),
        ],
        out_specs=pl.BlockSpec((tm, D), lambda i: (i, 0)),
        out_shape=jax.ShapeDtypeStruct((T, D), _F32),
        name="out_proj",
        compiler_params=_params("parallel"),
    )(ya.reshape(T, DA_WIDTH), yn.reshape(T, NA_WIDTH), ym.reshape(T, MLP_WIDTH), w_bf16, x2d,
      gate, post_g.reshape(1, D))


def _rope_tables(seq):
    n = DA_QK_DIM // 4
    freqs = ROPE_BASE ** (-np.arange(n, dtype=np.float64) / n)
    pos = np.arange(seq)
    row = (pos // GRID_W).astype(np.float64)[:, None] * freqs
    col = (pos % GRID_W).astype(np.float64)[:, None] * freqs
    zeros = np.zeros((seq, n))

    def one_map(fn_first, fn_second):
        return np.concatenate([fn_first(row), fn_second(row), fn_first(col), fn_second(col)], axis=-1)

    cos = one_map(np.cos, np.cos)
    sa = one_map(lambda a: -np.sin(a), lambda a: zeros)
    sb = one_map(lambda a: zeros, np.sin)
    return tuple(jnp.asarray(np.concatenate([t, t], axis=-1), _F32) for t in (cos, sa, sb))


def _identity_rope_tables(seq):
    return (jnp.ones((seq, HEAD_DIM), _F32), jnp.zeros((seq, HEAD_DIM), _F32),
            jnp.zeros((seq, HEAD_DIM), _F32))


def kernel(x, c, ctx, c_ctx, ada_w, ada_b, pre_g, post_g, w_in, w_out, lam_q1, lam_k1, lam_q2, lam_k2,
           da_g, na_rpb, na_g, mlp_ln_g, mlp_ln_b, mlp_ws, mlp_bs, mlp_g):
    B, S, D = x.shape
    C = ctx.shape[1]
    L = ada_w.shape[0]
    assert S % (NA_BLOCKS_PER_STEP * NA_Q) == 0 and S // GRID_W >= NA_KEY_ROWS and C % CHUNK == 0

    cc = jnp.concatenate([c, c_ctx[None], jnp.zeros((8 - B - 1, D), _F32)], axis=0)
    mod = _ada(cc, ada_w, ada_b)

    rope = _rope_tables(S)
    rope_id = _identity_rope_tables(C)
    w_in_b = w_in.astype(_BF16)
    w_out_b = w_out.astype(_BF16)
    x2d = x.reshape(B * S, D)
    xc2d = ctx.reshape(B * C, D)

    for l in range(L):
        lam_init = 0.8 - 0.6 * math.exp(-0.3 * l)
        ctx_out = l < L - 1
        shift, scale, gate = (mod[l, :B, i * D:(i + 1) * D].reshape(B, 1, D) for i in range(3))
        shift_c, scale_c, gate_c = (mod[l, B:B + 1, i * D:(i + 1) * D].reshape(1, 1, D) for i in range(3))
        lam_params = jnp.stack([lam_q1[l], lam_k1[l], lam_q2[l], lam_k2[l]])

        proj = _in_proj(x2d, pre_g[l], scale, shift, w_in_b, l, *rope, seq=S).reshape(B, S, IN_WIDTH)
        proj_c = _in_proj(xc2d, pre_g[l], scale_c, shift_c, w_in_b, l, *rope_id, seq=C).reshape(B, C, IN_WIDTH)

        ya = _diff_attn(proj, [proj, proj_c], lam_params, da_g[l], lam_init=lam_init)
        yn = _nbr_attn(proj, proj_c, _na_bias_tables(na_rpb[l], S // GRID_W), na_g[l])
        ym = _gmlp(proj, mlp_ln_g[l], mlp_ln_b[l], mlp_ws[l], mlp_bs[l], mlp_g[l], rows_per_step=512)
        x2d_new = _out_proj(ya, yn, ym, w_out_b, l, x2d, gate, post_g[l])

        if ctx_out:
            yca = _diff_attn(proj_c, [proj_c], lam_params, da_g[l], lam_init=lam_init)
            ycn = _ctx_attn(proj_c, na_g[l])
            ycm = _gmlp(proj_c, mlp_ln_g[l], mlp_ln_b[l], mlp_ws[l], mlp_bs[l], mlp_g[l], rows_per_step=C)
            xc2d = _out_proj(yca, ycn, ycm, w_out_b, l, xc2d, gate_c, post_g[l])
        x2d = x2d_new

    return x2d.reshape(B, S, D)
```

```python
import functools
import math

import numpy as np
import jax
import jax.numpy as jnp
from jax import lax
from jax.experimental import pallas as pl
from jax.experimental.pallas import tpu as pltpu

D_MODEL = 2048
DEPTH = 2
GRID_W = 64
HEAD_DIM = 128
DA_HEADS = 6
NA_HEADS = 6
MLP_GROUPS = 4
DA_WIDTH = DA_HEADS * HEAD_DIM
NA_WIDTH = NA_HEADS * HEAD_DIM
MLP_WIDTH = MLP_GROUPS * HEAD_DIM
DA_QK_DIM = HEAD_DIM // 2
NA_WIN_H = 8
NA_WIN_W = 16
CHUNK = 128
ROPE_BASE = 10000.0
EPS = 1e-6
IN_WIDTH = 4 * DA_WIDTH + 4 * NA_WIDTH + 3 * MLP_WIDTH
MIX_WIDTH = DA_WIDTH + NA_WIDTH + MLP_WIDTH

COL_AQ, COL_AK, COL_AV, COL_AZ = 0, 6, 12, 18
COL_BQ, COL_BK, COL_BV, COL_BZ = 24, 30, 36, 42
COL_CU, COL_CV, COL_CZ = 12, 13, 14

LOG2E = math.log2(math.e)
DA_SCALE = DA_QK_DIM ** -0.5 * LOG2E
NA_SCALE = HEAD_DIM ** -0.5 * LOG2E
DA_KEY_CHUNK = 512
DA_Q_TILE = 128
DA_TILES_PER_STEP = 8


def _in_segments():
    kinds = ["rope_scale", "rope", "plain", "plain", "na_scale", "plain", "plain", "plain", "gelu", "gelu", "plain"]
    widths = [DA_WIDTH] * 4 + [NA_WIDTH] * 4 + [MLP_WIDTH] * 3
    segs, c0 = [], 0
    for width, kind in zip(widths, kinds):
        segs.append((c0, width, kind))
        c0 += width
    return tuple(segs)


IN_SEGMENTS = _in_segments()
IN_TILE_ROWS = 256
OUT_TILE_ROWS = 512
OUT_SUB_ROWS = 256

NA_BLOCK_ROWS = 4
NA_KEY_ROWS = NA_BLOCK_ROWS + NA_WIN_H
NA_BLOCKS_PER_STEP = 2
NA_Q = NA_BLOCK_ROWS * GRID_W
NA_K = NA_KEY_ROWS * GRID_W
MASK_VALUE = -1e30

VMEM_LIMIT_BYTES = 56 * 1024 * 1024

_BF16 = jnp.bfloat16
_F32 = jnp.float32


def _silu(x):
    return x * (1.0 / (1.0 + jnp.exp(-x)))


def _gelu_tanh(x):
    c = math.sqrt(2.0 / math.pi)
    return 0.5 * x * (1.0 + jnp.tanh(c * (x + 0.044715 * (x * x * x))))


def _rms(x, g):
    return x * lax.rsqrt(jnp.mean(x * x, axis=-1, keepdims=True) + EPS) * g


def _dot_nt(a, b):
    return lax.dot_general(a, b, (((1,), (1,)), ((), ())), preferred_element_type=_F32)


def _params(*sem):
    return pltpu.CompilerParams(dimension_semantics=sem, vmem_limit_bytes=VMEM_LIMIT_BYTES)


def _ada_kernel(c_ref, w_ref, b_ref, o_ref):
    sc = _silu(c_ref[...]).astype(_BF16)
    o_ref[...] = jnp.dot(sc, w_ref[...].astype(_BF16), preferred_element_type=_F32) + b_ref[...]


def _ada(cc, ada_w, ada_b):
    L, D, N = ada_w.shape
    R = cc.shape[0]
    tn = 768
    return pl.pallas_call(
        _ada_kernel,
        grid=(L, N // tn),
        in_specs=[
            pl.BlockSpec((R, D), lambda l, j: (0, 0)),
            pl.BlockSpec((None, D, tn), lambda l, j: (l, 0, j)),
            pl.BlockSpec((None, 1, tn), lambda l, j: (l, 0, j)),
        ],
        out_specs=pl.BlockSpec((None, R, tn), lambda l, j: (l, 0, j)),
        out_shape=jax.ShapeDtypeStruct((L, R, N), _F32),
        name="ada_mod",
        compiler_params=_params("parallel", "parallel"),
    )(cc, ada_w, ada_b.reshape(L, 1, N))


def _in_kernel(x_ref, g_ref, scale_ref, shift_ref, w_ref, cos_ref, sa_ref, sb_ref, o_ref):
    x = x_ref[...]
    h = (_rms(x, g_ref[...]) * (1.0 + scale_ref[...]) + shift_ref[...]).astype(_BF16)
    cos, sa, sb = cos_ref[...], sa_ref[...], sb_ref[...]
    for c0, width, epilogue in IN_SEGMENTS:
        acc = jnp.dot(h, w_ref[:, c0:c0 + width], preferred_element_type=_F32)
        if epilogue in ("rope", "rope_scale"):
            for hh in range(width // HEAD_DIM):
                xs = acc[:, hh * HEAD_DIM:(hh + 1) * HEAD_DIM]
                r = xs * cos + pltpu.roll(xs, HEAD_DIM - 16, 1) * sa + pltpu.roll(xs, 16, 1) * sb
                if epilogue == "rope_scale":
                    r = r * DA_SCALE
                o_ref[:, c0 + hh * HEAD_DIM:c0 + (hh + 1) * HEAD_DIM] = r.astype(_BF16)
        elif epilogue == "na_scale":
            o_ref[:, c0:c0 + width] = (acc * NA_SCALE).astype(_BF16)
        elif epilogue == "gelu":
            o_ref[:, c0:c0 + width] = _gelu_tanh(acc).astype(_BF16)
        else:
            o_ref[:, c0:c0 + width] = acc.astype(_BF16)


def _in_proj(x2d, pre_g, scale, shift, w_bf16, layer, cos, sa, sb, *, seq):
    T, D = x2d.shape
    nb = scale.shape[0]
    tm = IN_TILE_ROWS
    tiles_per_mod = (T // nb) // tm
    tiles_per_seq = seq // tm
    return pl.pallas_call(
        _in_kernel,
        grid=(T // tm,),
        in_specs=[
            pl.BlockSpec((tm, D), lambda i: (i, 0)),
            pl.BlockSpec((1, D), lambda i: (0, 0)),
            pl.BlockSpec((None, 1, D), lambda i: (i // tiles_per_mod, 0, 0)),
            pl.BlockSpec((None, 1, D), lambda i: (i // tiles_per_mod, 0, 0)),
            pl.BlockSpec((None, D, IN_WIDTH), lambda i: (layer, 0, 0), pipeline_mode=pl.Buffered(1)),
            pl.BlockSpec((tm, HEAD_DIM), lambda i: (i % tiles_per_seq, 0)),
            pl.BlockSpec((tm, HEAD_DIM), lambda i: (i % tiles_per_seq, 0)),
            pl.BlockSpec((tm, HEAD_DIM), lambda i: (i % tiles_per_seq, 0)),
        ],
        out_specs=pl.BlockSpec((tm, IN_WIDTH), lambda i: (i, 0)),
        out_shape=jax.ShapeDtypeStruct((T, IN_WIDTH), _BF16),
        name="in_proj",
        compiler_params=_params("parallel"),
    )(x2d, pre_g.reshape(1, D), scale, shift, w_bf16, cos, sa, sb)


def _softmax_pv(qq, sources):
    scores = []
    for k, _, bias in sources:
        s = _dot_nt(qq, k)
        if bias is not None:
            s = s + bias
        scores.append(s)
    m = jnp.max(scores[0], axis=-1, keepdims=True)
    for s in scores[1:]:
        m = jnp.maximum(m, jnp.max(s, axis=-1, keepdims=True))
    acc = None
    for s, (_, v, _) in zip(scores, sources):
        p = jnp.exp2(s - m).astype(_BF16)
        v_aug = jnp.concatenate([v, jnp.ones(v.shape, _BF16)], axis=1)
        pv = jnp.dot(p, v_aug, preferred_element_type=_F32)
        acc = pv if acc is None else acc + pv
    return acc[:, :HEAD_DIM] / acc[:, HEAD_DIM:]


def _branch_out(o, g, factor, z_ref, o_ref):
    z = z_ref[...].astype(_F32)
    o_ref[...] = (_rms(o, g) * factor * _silu(z)).astype(_BF16)


def _da_kernel(*refs, n_src, lam_init):
    q_ref, z_ref = refs[0], refs[1]
    kv_refs = refs[2:2 + 2 * n_src]
    lam_ref, g_ref, o_ref, s_ref = refs[2 + 2 * n_src:]
    tq = DA_Q_TILE
    n_tiles = q_ref.shape[0] // tq

    chunks = []
    col = 0
    for i in range(n_src):
        k_ref, v_ref = kv_refs[2 * i], kv_refs[2 * i + 1]
        size = k_ref.shape[0]
        ck = min(size, DA_KEY_CHUNK)
        for c in range(size // ck):
            chunks.append((k_ref, v_ref, c * ck, ck, col))
            col += ck

    lp = lam_ref[...]
    lam = (jnp.exp(jnp.sum(lp[0:1] * lp[1:2], axis=-1, keepdims=True))
           - jnp.exp(jnp.sum(lp[2:3] * lp[3:4], axis=-1, keepdims=True)) + lam_init)

    def scores_and_max(t):
        q = q_ref[t * tq:(t + 1) * tq, :]
        lane = lax.broadcasted_iota(jnp.int32, q.shape, 1)
        zero = jnp.zeros_like(q)
        qq = jnp.concatenate([jnp.where(lane < DA_QK_DIM, q, zero),
                              jnp.where(lane >= DA_QK_DIM, q, zero)], axis=0)
        mx = None
        for k_ref, _, r0, ck, c0 in chunks:
            s = _dot_nt(qq, k_ref[r0:r0 + ck, :])
            s_ref[t % 2, :, c0:c0 + ck] = s
            for jj in range(ck // HEAD_DIM):
                part = s[:, jj * HEAD_DIM:(jj + 1) * HEAD_DIM]
                mx = part if mx is None else jnp.maximum(mx, part)
        return jnp.max(mx, axis=-1, keepdims=True)

    def softmax_pv_out(t, m):
        acc = None
        for _, v_ref, r0, ck, c0 in chunks:
            p = jnp.exp2(s_ref[t % 2, :, c0:c0 + ck] - m).astype(_BF16)
            v_aug = jnp.concatenate([v_ref[r0:r0 + ck, :], jnp.ones((ck, HEAD_DIM), _BF16)], axis=1)
            pv = jnp.dot(p, v_aug, preferred_element_type=_F32)
            acc = pv if acc is None else acc + pv
        o = acc[:, :HEAD_DIM] / acc[:, HEAD_DIM:]
        oa = o[:tq] - lam * o[tq:]
        rows = slice(t * tq, (t + 1) * tq)
        z = z_ref[rows, :].astype(_F32)
        o_ref[rows, :] = (_rms(oa, g_ref[...]) * (1.0 - lam_init) * _silu(z)).astype(_BF16)

    row_max = scores_and_max(0)
    for t in range(1, n_tiles):
        next_max = scores_and_max(t)
        softmax_pv_out(t - 1, row_max)
        row_max = next_max
    softmax_pv_out(n_tiles - 1, row_max)


def _diff_attn(q_proj, kv_projs, lam_params, da_g, *, lam_init):
    B, Sq, _ = q_proj.shape
    tq = min(DA_TILES_PER_STEP * DA_Q_TILE, Sq)
    in_specs = [
        pl.BlockSpec((None, tq, HEAD_DIM), lambda b, h, i: (b, i, COL_AQ + h)),
        pl.BlockSpec((None, tq, HEAD_DIM), lambda b, h, i: (b, i, COL_AZ + h)),
    ]
    args = [q_proj, q_proj]
    for kp in kv_projs:
        sk = kp.shape[1]
        in_specs.append(pl.BlockSpec((None, sk, HEAD_DIM), lambda b, h, i: (b, 0, COL_AK + h)))
        in_specs.append(pl.BlockSpec((None, sk, HEAD_DIM), lambda b, h, i: (b, 0, COL_AV + h)))
        args += [kp, kp]
    in_specs += [
        pl.BlockSpec((4, DA_QK_DIM), lambda b, h, i: (0, 0)),
        pl.BlockSpec((1, HEAD_DIM), lambda b, h, i: (0, 0)),
    ]
    args += [lam_params, da_g.reshape(1, HEAD_DIM)]
    return pl.pallas_call(
        functools.partial(_da_kernel, n_src=len(kv_projs), lam_init=lam_init),
        grid=(B, DA_HEADS, Sq // tq),
        in_specs=in_specs,
        out_specs=pl.BlockSpec((None, tq, HEAD_DIM), lambda b, h, i: (b, i, h)),
        out_shape=jax.ShapeDtypeStruct((B, Sq, DA_WIDTH), _BF16),
        scratch_shapes=[pltpu.VMEM((2, 2 * DA_Q_TILE, sum(kp.shape[1] for kp in kv_projs)), _F32)],
        name="diff_attn",
        compiler_params=_params("parallel", "parallel", "parallel"),
    )(*args)


def _na_kernel(q_ref, z_ref, k_ref, v_ref, kc_ref, vc_ref, *rest):
    bias_refs, (g_ref, o_ref) = rest[:NA_BLOCKS_PER_STEP], rest[NA_BLOCKS_PER_STEP:]
    step = pl.program_id(1)
    rows = k_ref.shape[0] // GRID_W
    n_ctx = kc_ref.shape[0]
    ones = jnp.ones((NA_K, HEAD_DIM), _BF16)

    def key_start(j):
        rb = step * NA_BLOCKS_PER_STEP + j
        kr0 = jnp.clip(rb * NA_BLOCK_ROWS - NA_WIN_H // 2, 0, rows - NA_KEY_ROWS)
        return pl.multiple_of(kr0 * GRID_W, GRID_W)

    def scores_and_max(j, h):
        cols = slice(h * HEAD_DIM, (h + 1) * HEAD_DIM)
        q = q_ref[j * NA_Q:(j + 1) * NA_Q, cols]
        s_loc = _dot_nt(q, k_ref[pl.ds(key_start(j), NA_K), cols]) + bias_refs[j][h]
        s_ctx = _dot_nt(q, kc_ref[:, cols])
        m = jnp.maximum(jnp.max(s_loc, axis=-1, keepdims=True), jnp.max(s_ctx, axis=-1, keepdims=True))
        return s_loc, s_ctx, m

    def softmax_pv_out(j, h, s_loc, s_ctx, m):
        cols = slice(h * HEAD_DIM, (h + 1) * HEAD_DIM)
        v_loc = jnp.concatenate([v_ref[pl.ds(key_start(j), NA_K), cols], ones], axis=1)
        v_ctx = jnp.concatenate([vc_ref[:, cols], ones[:n_ctx]], axis=1)
        acc = jnp.dot(jnp.exp2(s_loc - m).astype(_BF16), v_loc, preferred_element_type=_F32)
        acc = acc + jnp.dot(jnp.exp2(s_ctx - m).astype(_BF16), v_ctx, preferred_element_type=_F32)
        o = acc[:, :HEAD_DIM] / acc[:, HEAD_DIM:]
        qrows = slice(j * NA_Q, (j + 1) * NA_Q)
        z = z_ref[qrows, cols].astype(_F32)
        o_ref[qrows, cols] = (_rms(o, g_ref[:, cols]) * _silu(z)).astype(_BF16)

    units = [(j, h) for j in range(NA_BLOCKS_PER_STEP) for h in range(NA_HEADS)]
    pending = scores_and_max(*units[0])
    for u in range(1, len(units)):
        nxt = scores_and_max(*units[u])
        softmax_pv_out(*units[u - 1], *pending)
        pending = nxt
    softmax_pv_out(*units[-1], *pending)


def _na_bias_kernel(r_ref, o_ref, *, rows):
    shape = (GRID_W, 2 * GRID_W)
    c = lax.broadcasted_iota(jnp.int32, shape, 0)
    lane = lax.broadcasted_iota(jnp.int32, shape, 1)
    kc = lane & (GRID_W - 1)
    cs = jnp.clip(c - NA_WIN_W // 2, 0, GRID_W - NA_WIN_W)
    col_valid = (kc >= cs) & (kc < cs + NA_WIN_W)
    first_key_row = lane < GRID_W
    toeplitz = []
    for t in range(2 * NA_WIN_H):
        row = jnp.broadcast_to(r_ref[t:t + 1, :], shape)
        x = pltpu.roll(row, 2 * GRID_W - (NA_WIN_W - 1), 1, stride=1, stride_axis=0)
        toeplitz.append(jnp.where(col_valid, x * LOG2E, MASK_VALUE))
    masked = jnp.full(shape, MASK_VALUE, _F32)
    for cfg, (r0, kr0) in enumerate(_na_block_configs(rows)):
        for ri in range(NA_BLOCK_ROWS):
            r = r0 + ri
            rs = min(max(r - NA_WIN_H // 2, 0), rows - NA_WIN_H)
            for j in range(NA_KEY_ROWS // 2):
                kra = kr0 + 2 * j
                valid_a = rs <= kra < rs + NA_WIN_H
                valid_b = rs <= kra + 1 < rs + NA_WIN_H
                t = kra - r + NA_WIN_H
                if valid_a and valid_b:
                    blk = toeplitz[t]
                elif valid_a:
                    blk = jnp.where(first_key_row, toeplitz[t], MASK_VALUE)
                elif valid_b:
                    blk = jnp.where(first_key_row, MASK_VALUE, toeplitz[t])
                else:
                    blk = masked
                o_ref[cfg, ri * GRID_W:(ri + 1) * GRID_W, j * 2 * GRID_W:(j + 1) * 2 * GRID_W] = blk


def _na_block_configs(rows):
    return ((0, 0), (2 * NA_BLOCK_ROWS, 2 * NA_BLOCK_ROWS - NA_WIN_H // 2),
            (rows - NA_BLOCK_ROWS, rows - NA_KEY_ROWS))


def _na_bias_tables(rpb, rows):
    H, nr, ncol = rpb.shape
    lo = jnp.pad(rpb, ((0, 0), (1, 0), (0, 2 * GRID_W - ncol)))
    hi = jnp.pad(rpb, ((0, 0), (0, 1), (GRID_W, GRID_W - ncol)))
    return pl.pallas_call(
        functools.partial(_na_bias_kernel, rows=rows),
        grid=(H,),
        in_specs=[pl.BlockSpec((None, nr + 1, 2 * GRID_W), lambda h: (h, 0, 0))],
        out_specs=pl.BlockSpec((3, None, NA_Q, NA_K), lambda h: (0, h, 0, 0)),
        out_shape=jax.ShapeDtypeStruct((3, H, NA_Q, NA_K), _F32),
        name="na_bias",
        compiler_params=_params("parallel"),
    )(lo + hi)


def _nbr_attn(proj, proj_ctx, bias_tabs, na_g):
    B, S, _ = proj.shape
    C = proj_ctx.shape[1]
    n_blocks = S // NA_Q
    step_q = NA_BLOCKS_PER_STEP * NA_Q

    def bias_spec(j):
        def idx(b, i):
            blk = i * NA_BLOCKS_PER_STEP + j
            return (jnp.where(blk == 0, 0, jnp.where(blk == n_blocks - 1, 2, 1)), 0, 0, 0)
        return pl.BlockSpec((None, NA_HEADS, NA_Q, NA_K), idx)

    cq, ck, cv, cz = (c * HEAD_DIM // NA_WIDTH for c in (COL_BQ, COL_BK, COL_BV, COL_BZ))
    return pl.pallas_call(
        _na_kernel,
        grid=(B, n_blocks // NA_BLOCKS_PER_STEP),
        in_specs=[
            pl.BlockSpec((None, step_q, NA_WIDTH), lambda b, i: (b, i, cq)),
            pl.BlockSpec((None, step_q, NA_WIDTH), lambda b, i: (b, i, cz)),
            pl.BlockSpec((None, S, NA_WIDTH), lambda b, i: (b, 0, ck), pipeline_mode=pl.Buffered(1)),
            pl.BlockSpec((None, S, NA_WIDTH), lambda b, i: (b, 0, cv), pipeline_mode=pl.Buffered(1)),
            pl.BlockSpec((None, C, NA_WIDTH), lambda b, i: (b, 0, ck)),
            pl.BlockSpec((None, C, NA_WIDTH), lambda b, i: (b, 0, cv)),
            *[bias_spec(j) for j in range(NA_BLOCKS_PER_STEP)],
            pl.BlockSpec((1, NA_WIDTH), lambda b, i: (0, 0)),
        ],
        out_specs=pl.BlockSpec((None, step_q, NA_WIDTH), lambda b, i: (b, i, 0)),
        out_shape=jax.ShapeDtypeStruct((B, S, NA_WIDTH), _BF16),
        name="nbr_attn",
        compiler_params=_params("parallel", "parallel"),
    )(proj, proj, proj, proj, proj_ctx, proj_ctx, *([bias_tabs] * NA_BLOCKS_PER_STEP), na_g.reshape(1, NA_WIDTH))


def _ctx_attn_kernel(q_ref, z_ref, k_ref, v_ref, g_ref, o_ref):
    o = _softmax_pv(q_ref[...], [(k_ref[...], v_ref[...], None)])
    _branch_out(o, g_ref[...], 1.0, z_ref, o_ref)


def _ctx_attn(proj_ctx, na_g):
    B, C, _ = proj_ctx.shape
    return pl.pallas_call(
        _ctx_attn_kernel,
        grid=(B, NA_HEADS),
        in_specs=[
            pl.BlockSpec((None, C, HEAD_DIM), lambda b, h: (b, 0, COL_BQ + h)),
            pl.BlockSpec((None, C, HEAD_DIM), lambda b, h: (b, 0, COL_BZ + h)),
            pl.BlockSpec((None, C, HEAD_DIM), lambda b, h: (b, 0, COL_BK + h)),
            pl.BlockSpec((None, C, HEAD_DIM), lambda b, h: (b, 0, COL_BV + h)),
            pl.BlockSpec((None, 1, HEAD_DIM), lambda b, h: (h, 0, 0)),
        ],
        out_specs=pl.BlockSpec((None, C, HEAD_DIM), lambda b, h: (b, 0, h)),
        out_shape=jax.ShapeDtypeStruct((B, C, NA_WIDTH), _BF16),
        name="ctx_attn",
        compiler_params=_params("parallel", "parallel"),
    )(proj_ctx, proj_ctx, proj_ctx, proj_ctx, na_g.reshape(NA_HEADS, 1, HEAD_DIM))


def _gmlp_rows(rows, u_ref, v_ref, z_ref, lng_ref, lnb_ref, ws_ref, bs_ref, g_ref):
    n_chunks = (rows.stop - rows.start) // CHUNK
    groups = []
    for g in range(MLP_GROUPS):
        cols = slice(g * HEAD_DIM, (g + 1) * HEAD_DIM)
        w = ws_ref[g].astype(_BF16)
        bs = jnp.concatenate([bs_ref[g]] * n_chunks, axis=0)
        v = v_ref[rows, cols].astype(_F32)
        vc = v - jnp.mean(v, axis=-1, keepdims=True)
        vn = vc * lax.rsqrt(jnp.mean(vc * vc, axis=-1, keepdims=True) + EPS) * lng_ref[:, cols] + lnb_ref[:, cols]
        vn = vn.astype(_BF16)
        mixed = jnp.concatenate(
            [jnp.dot(w, vn[c * CHUNK:(c + 1) * CHUNK, :], preferred_element_type=_F32) for c in range(n_chunks)],
            axis=0) + bs
        om = u_ref[rows, cols].astype(_F32) * mixed
        z = z_ref[rows, cols].astype(_F32)
        groups.append((_rms(om, g_ref[:, cols]) * _silu(z)).astype(_BF16))
    return jnp.concatenate(groups, axis=1)


def _out_kernel(ya_ref, yn_ref, u_ref, v_ref, z_ref, lng_ref, lnb_ref, ws_ref, bs_ref, mg_ref,
                w_ref, x_ref, gate_ref, g_ref, o_ref):
    sub_tiles = [slice(r0, r0 + OUT_SUB_ROWS) for r0 in range(0, x_ref.shape[0], OUT_SUB_ROWS)]
    yms = [_gmlp_rows(rows, u_ref, v_ref, z_ref, lng_ref, lnb_ref, ws_ref, bs_ref, mg_ref) for rows in sub_tiles]
    for rows, ym in zip(sub_tiles, yms):
        y = jnp.dot(ya_ref[rows, :], w_ref[:DA_WIDTH, :], preferred_element_type=_F32)
        y = y + jnp.dot(yn_ref[rows, :], w_ref[DA_WIDTH:DA_WIDTH + NA_WIDTH, :], preferred_element_type=_F32)
        y = y + jnp.dot(ym, w_ref[DA_WIDTH + NA_WIDTH:, :], preferred_element_type=_F32)
        o_ref[rows, :] = x_ref[rows, :] + gate_ref[...] * _rms(y, g_ref[...])


def _out_proj(ya, yn, proj, mlp_params, w_bf16, layer, x2d, gate, post_g):
    T, D = x2d.shape
    nb = gate.shape[0]
    tm = min(OUT_TILE_ROWS, T // nb)
    tiles_per_mod = (T // nb) // tm
    ln_g, ln_b, ws, bs, out_g = mlp_params
    proj2d = proj.reshape(T, IN_WIDTH)
    vec = pl.BlockSpec((1, MLP_WIDTH), lambda i: (0, 0))
    return pl.pallas_call(
        _out_kernel,
        grid=(T // tm,),
        in_specs=[
            pl.BlockSpec((tm, DA_WIDTH), lambda i: (i, 0)),
            pl.BlockSpec((tm, NA_WIDTH), lambda i: (i, 0)),
            pl.BlockSpec((tm, MLP_WIDTH), lambda i: (i, COL_CU)),
            pl.BlockSpec((tm, MLP_WIDTH), lambda i: (i, COL_CV)),
            pl.BlockSpec((tm, MLP_WIDTH), lambda i: (i, COL_CZ)),
            vec,
            vec,
            pl.BlockSpec((MLP_GROUPS, CHUNK, CHUNK), lambda i: (0, 0, 0)),
            pl.BlockSpec((MLP_GROUPS, CHUNK, 1), lambda i: (0, 0, 0)),
            vec,
            pl.BlockSpec((None, MIX_WIDTH, D), lambda i: (layer, 0, 0), pipeline_mode=pl.Buffered(1)),
            pl.BlockSpec((tm, D), lambda i: (i, 0)),
            pl.BlockSpec((None, 1, D), lambda i: (i // tiles_per_mod, 0, 0)),
            pl.BlockSpec((1, D), lambda i: (0, 0)),
        ],
        out_specs=pl.BlockSpec((tm, D), lambda i: (i, 0)),
        out_shape=jax.ShapeDtypeStruct((T, D), _F32),
        name="out_proj",
        compiler_params=_params("parallel"),
    )(ya.reshape(T, DA_WIDTH), yn.reshape(T, NA_WIDTH), proj2d, proj2d, proj2d,
      ln_g.reshape(1, MLP_WIDTH), ln_b.reshape(1, MLP_WIDTH), ws, bs.reshape(MLP_GROUPS, CHUNK, 1),
      out_g.reshape(1, MLP_WIDTH), w_bf16, x2d, gate, post_g.reshape(1, D))


def _rope_tables(seq):
    n = DA_QK_DIM // 4
    freqs = ROPE_BASE ** (-np.arange(n, dtype=np.float64) / n)
    pos = np.arange(seq)
    row = (pos // GRID_W).astype(np.float64)[:, None] * freqs
    col = (pos % GRID_W).astype(np.float64)[:, None] * freqs
    zeros = np.zeros((seq, n))

    def one_map(fn_first, fn_second):
        return np.concatenate([fn_first(row), fn_second(row), fn_first(col), fn_second(col)], axis=-1)

    cos = one_map(np.cos, np.cos)
    sa = one_map(lambda a: -np.sin(a), lambda a: zeros)
    sb = one_map(lambda a: zeros, np.sin)
    return tuple(jnp.asarray(np.concatenate([t, t], axis=-1), _F32) for t in (cos, sa, sb))


def _identity_rope_tables(seq):
    return (jnp.ones((seq, HEAD_DIM), _F32), jnp.zeros((seq, HEAD_DIM), _F32),
            jnp.zeros((seq, HEAD_DIM), _F32))


def kernel(x, c, ctx, c_ctx, ada_w, ada_b, pre_g, post_g, w_in, w_out, lam_q1, lam_k1, lam_q2, lam_k2,
           da_g, na_rpb, na_g, mlp_ln_g, mlp_ln_b, mlp_ws, mlp_bs, mlp_g):
    B, S, D = x.shape
    C = ctx.shape[1]
    L = ada_w.shape[0]
    assert S % (NA_BLOCKS_PER_STEP * NA_Q) == 0 and S // GRID_W >= NA_KEY_ROWS and C % CHUNK == 0

    cc = jnp.concatenate([c, c_ctx[None], jnp.zeros((8 - B - 1, D), _F32)], axis=0)
    mod = _ada(cc, ada_w, ada_b)

    rope = _rope_tables(S)
    rope_id = _identity_rope_tables(C)
    w_in_b = w_in.astype(_BF16)
    w_out_b = w_out.astype(_BF16)
    x2d = x.reshape(B * S, D)
    xc2d = ctx.reshape(B * C, D)

    for l in range(L):
        lam_init = 0.8 - 0.6 * math.exp(-0.3 * l)
        ctx_out = l < L - 1
        shift, scale, gate = (mod[l, :B, i * D:(i + 1) * D].reshape(B, 1, D) for i in range(3))
        shift_c, scale_c, gate_c = (mod[l, B:B + 1, i * D:(i + 1) * D].reshape(1, 1, D) for i in range(3))
        lam_params = jnp.stack([lam_q1[l], lam_k1[l], lam_q2[l], lam_k2[l]])

        proj = _in_proj(x2d, pre_g[l], scale, shift, w_in_b, l, *rope, seq=S).reshape(B, S, IN_WIDTH)
        proj_c = _in_proj(xc2d, pre_g[l], scale_c, shift_c, w_in_b, l, *rope_id, seq=C).reshape(B, C, IN_WIDTH)

        ya = _diff_attn(proj, [proj, proj_c], lam_params, da_g[l], lam_init=lam_init)
        yn = _nbr_attn(proj, proj_c, _na_bias_tables(na_rpb[l], S // GRID_W), na_g[l])
        mlp_params = (mlp_ln_g[l], mlp_ln_b[l], mlp_ws[l], mlp_bs[l], mlp_g[l])
        x2d_new = _out_proj(ya, yn, proj, mlp_params, w_out_b, l, x2d, gate, post_g[l])

        if ctx_out:
            yca = _diff_attn(proj_c, [proj_c], lam_params, da_g[l], lam_init=lam_init)
            ycn = _ctx_attn(proj_c, na_g[l])
            xc2d = _out_proj(yca, ycn, proj_c, mlp_params, w_out_b, l, xc2d, gate_c, post_g[l])
        x2d = x2d_new

    return x2d.reshape(B, S, D)
```

```python
import functools
import math

import numpy as np
import jax
import jax.numpy as jnp
from jax import lax
from jax.experimental import pallas as pl
from jax.experimental.pallas import tpu as pltpu

D_MODEL = 2048
DEPTH = 2
GRID_W = 64
HEAD_DIM = 128
DA_HEADS = 6
NA_HEADS = 6
MLP_GROUPS = 4
DA_WIDTH = DA_HEADS * HEAD_DIM
NA_WIDTH = NA_HEADS * HEAD_DIM
MLP_WIDTH = MLP_GROUPS * HEAD_DIM
DA_QK_DIM = HEAD_DIM // 2
NA_WIN_H = 8
NA_WIN_W = 16
CHUNK = 128
ROPE_BASE = 10000.0
EPS = 1e-6
IN_WIDTH = 4 * DA_WIDTH + 4 * NA_WIDTH + 3 * MLP_WIDTH
MIX_WIDTH = DA_WIDTH + NA_WIDTH + MLP_WIDTH

COL_AQ, COL_AK, COL_AV, COL_AZ = 0, 6, 12, 18
COL_BQ, COL_BK, COL_BV, COL_BZ = 24, 30, 36, 42
COL_CU, COL_CV, COL_CZ = 12, 13, 14

LOG2E = math.log2(math.e)
DA_SCALE = DA_QK_DIM ** -0.5 * LOG2E
NA_SCALE = HEAD_DIM ** -0.5 * LOG2E
DA_KEY_CHUNK = 512
DA_Q_TILE = 128
DA_TILES_PER_STEP = 16


def _in_segments():
    kinds = ["rope_scale", "rope", "plain", "plain", "na_scale", "plain", "plain", "plain", "gelu", "gelu", "plain"]
    widths = [DA_WIDTH] * 4 + [NA_WIDTH] * 4 + [MLP_WIDTH] * 3
    segs, c0 = [], 0
    for width, kind in zip(widths, kinds):
        segs.append((c0, width, kind))
        c0 += width
    return tuple(segs)


IN_SEGMENTS = _in_segments()
IN_KV_SEGMENTS = (1, 2, 5, 6)
IN_TILE_ROWS = 256
OUT_TILE_ROWS = 512
OUT_SUB_ROWS = 256

NA_BLOCK_ROWS = 4
NA_KEY_ROWS = NA_BLOCK_ROWS + NA_WIN_H
NA_BLOCKS_PER_STEP = 2
NA_Q = NA_BLOCK_ROWS * GRID_W
NA_K = NA_KEY_ROWS * GRID_W
MASK_VALUE = -1e30

VMEM_LIMIT_BYTES = 56 * 1024 * 1024

_BF16 = jnp.bfloat16
_F32 = jnp.float32


def _silu(x):
    return x * (1.0 / (1.0 + jnp.exp(-x)))


def _gelu_tanh(x):
    c = math.sqrt(2.0 / math.pi)
    return 0.5 * x * (1.0 + jnp.tanh(c * (x + 0.044715 * (x * x * x))))


def _rms(x, g):
    return x * lax.rsqrt(jnp.mean(x * x, axis=-1, keepdims=True) + EPS) * g


def _dot_nt(a, b):
    return lax.dot_general(a, b, (((1,), (1,)), ((), ())), preferred_element_type=_F32)


def _params(*sem):
    return pltpu.CompilerParams(dimension_semantics=sem, vmem_limit_bytes=VMEM_LIMIT_BYTES)


def _ada_kernel(c_ref, w_ref, b_ref, o_ref):
    sc = _silu(c_ref[...]).astype(_BF16)
    o_ref[...] = jnp.dot(sc, w_ref[...].astype(_BF16), preferred_element_type=_F32) + b_ref[...]


def _ada(cc, ada_w, ada_b):
    L, D, N = ada_w.shape
    R = cc.shape[0]
    tn = 768
    return pl.pallas_call(
        _ada_kernel,
        grid=(L, N // tn),
        in_specs=[
            pl.BlockSpec((R, D), lambda l, j: (0, 0)),
            pl.BlockSpec((None, D, tn), lambda l, j: (l, 0, j)),
            pl.BlockSpec((None, 1, tn), lambda l, j: (l, 0, j)),
        ],
        out_specs=pl.BlockSpec((None, R, tn), lambda l, j: (l, 0, j)),
        out_shape=jax.ShapeDtypeStruct((L, R, N), _F32),
        name="ada_mod",
        compiler_params=_params("parallel", "parallel"),
    )(cc, ada_w, ada_b.reshape(L, 1, N))


def _in_kernel(x_ref, g_ref, scale_ref, shift_ref, w_ref, cos_ref, sa_ref, sb_ref, o_ref, *, kv_only):
    x = x_ref[...]
    h = (_rms(x, g_ref[...]) * (1.0 + scale_ref[...]) + shift_ref[...]).astype(_BF16)
    cos, sa, sb = cos_ref[...], sa_ref[...], sb_ref[...]
    for idx, (c0, width, epilogue) in enumerate(IN_SEGMENTS):
        if kv_only and idx not in IN_KV_SEGMENTS:
            o_ref[:, c0:c0 + width] = jnp.zeros((o_ref.shape[0], width), _BF16)
            continue
        acc = jnp.dot(h, w_ref[:, c0:c0 + width], preferred_element_type=_F32)
        if epilogue in ("rope", "rope_scale"):
            for hh in range(width // HEAD_DIM):
                xs = acc[:, hh * HEAD_DIM:(hh + 1) * HEAD_DIM]
                r = xs * cos + pltpu.roll(xs, HEAD_DIM - 16, 1) * sa + pltpu.roll(xs, 16, 1) * sb
                if epilogue == "rope_scale":
                    r = r * DA_SCALE
                o_ref[:, c0 + hh * HEAD_DIM:c0 + (hh + 1) * HEAD_DIM] = r.astype(_BF16)
        elif epilogue == "na_scale":
            o_ref[:, c0:c0 + width] = (acc * NA_SCALE).astype(_BF16)
        elif epilogue == "gelu":
            o_ref[:, c0:c0 + width] = _gelu_tanh(acc).astype(_BF16)
        else:
            o_ref[:, c0:c0 + width] = acc.astype(_BF16)


def _in_proj(x2d, pre_g, scale, shift, w_bf16, layer, cos, sa, sb, *, seq, kv_only=False):
    T, D = x2d.shape
    nb = scale.shape[0]
    tm = IN_TILE_ROWS
    tiles_per_mod = (T // nb) // tm
    tiles_per_seq = seq // tm
    return pl.pallas_call(
        functools.partial(_in_kernel, kv_only=kv_only),
        grid=(T // tm,),
        in_specs=[
            pl.BlockSpec((tm, D), lambda i: (i, 0)),
            pl.BlockSpec((1, D), lambda i: (0, 0)),
            pl.BlockSpec((None, 1, D), lambda i: (i // tiles_per_mod, 0, 0)),
            pl.BlockSpec((None, 1, D), lambda i: (i // tiles_per_mod, 0, 0)),
            pl.BlockSpec((None, D, IN_WIDTH), lambda i: (layer, 0, 0), pipeline_mode=pl.Buffered(1)),
            pl.BlockSpec((tm, HEAD_DIM), lambda i: (i % tiles_per_seq, 0)),
            pl.BlockSpec((tm, HEAD_DIM), lambda i: (i % tiles_per_seq, 0)),
            pl.BlockSpec((tm, HEAD_DIM), lambda i: (i % tiles_per_seq, 0)),
        ],
        out_specs=pl.BlockSpec((tm, IN_WIDTH), lambda i: (i, 0)),
        out_shape=jax.ShapeDtypeStruct((T, IN_WIDTH), _BF16),
        name="in_proj",
        compiler_params=_params("parallel"),
    )(x2d, pre_g.reshape(1, D), scale, shift, w_bf16, cos, sa, sb)


def _softmax_pv(qq, sources):
    scores = []
    for k, _, bias in sources:
        s = _dot_nt(qq, k)
        if bias is not None:
            s = s + bias
        scores.append(s)
    m = jnp.max(scores[0], axis=-1, keepdims=True)
    for s in scores[1:]:
        m = jnp.maximum(m, jnp.max(s, axis=-1, keepdims=True))
    acc = None
    for s, (_, v, _) in zip(scores, sources):
        p = jnp.exp2(s - m).astype(_BF16)
        v_aug = jnp.concatenate([v, jnp.ones(v.shape, _BF16)], axis=1)
        pv = jnp.dot(p, v_aug, preferred_element_type=_F32)
        acc = pv if acc is None else acc + pv
    return acc[:, :HEAD_DIM] / acc[:, HEAD_DIM:]


def _branch_out(o, g, factor, z_ref, o_ref):
    z = z_ref[...].astype(_F32)
    o_ref[...] = (_rms(o, g) * factor * _silu(z)).astype(_BF16)


def _da_kernel(*refs, n_src, lam_init):
    q_ref, z_ref = refs[0], refs[1]
    kv_refs = refs[2:2 + 2 * n_src]
    lam_ref, g_ref, o_ref, s_ref = refs[2 + 2 * n_src:]
    tq = DA_Q_TILE
    n_tiles = q_ref.shape[0] // tq

    chunks = []
    col = 0
    for i in range(n_src):
        k_ref, v_ref = kv_refs[2 * i], kv_refs[2 * i + 1]
        size = k_ref.shape[0]
        ck = min(size, DA_KEY_CHUNK)
        for c in range(size // ck):
            chunks.append((k_ref, v_ref, c * ck, ck, col))
            col += ck

    lp = lam_ref[...]
    lam = (jnp.exp(jnp.sum(lp[0:1] * lp[1:2], axis=-1, keepdims=True))
           - jnp.exp(jnp.sum(lp[2:3] * lp[3:4], axis=-1, keepdims=True)) + lam_init)

    def scores_and_max(t):
        q = q_ref[t * tq:(t + 1) * tq, :]
        lane = lax.broadcasted_iota(jnp.int32, q.shape, 1)
        zero = jnp.zeros_like(q)
        qq = jnp.concatenate([jnp.where(lane < DA_QK_DIM, q, zero),
                              jnp.where(lane >= DA_QK_DIM, q, zero)], axis=0)
        mx = None
        for k_ref, _, r0, ck, c0 in chunks:
            s = _dot_nt(qq, k_ref[r0:r0 + ck, :])
            s_ref[t % 2, :, c0:c0 + ck] = s
            for jj in range(ck // HEAD_DIM):
                part = s[:, jj * HEAD_DIM:(jj + 1) * HEAD_DIM]
                mx = part if mx is None else jnp.maximum(mx, part)
        return jnp.max(mx, axis=-1, keepdims=True)

    def softmax_pv_out(t, m):
        acc = None
        for _, v_ref, r0, ck, c0 in chunks:
            p = jnp.exp2(s_ref[t % 2, :, c0:c0 + ck] - m).astype(_BF16)
            v_aug = jnp.concatenate([v_ref[r0:r0 + ck, :], jnp.ones((ck, HEAD_DIM), _BF16)], axis=1)
            pv = jnp.dot(p, v_aug, preferred_element_type=_F32)
            acc = pv if acc is None else acc + pv
        o = acc[:, :HEAD_DIM] / acc[:, HEAD_DIM:]
        oa = o[:tq] - lam * o[tq:]
        rows = slice(t * tq, (t + 1) * tq)
        z = z_ref[rows, :].astype(_F32)
        o_ref[rows, :] = (_rms(oa, g_ref[...]) * (1.0 - lam_init) * _silu(z)).astype(_BF16)

    row_max = scores_and_max(0)
    for t in range(1, n_tiles):
        next_max = scores_and_max(t)
        softmax_pv_out(t - 1, row_max)
        row_max = next_max
    softmax_pv_out(n_tiles - 1, row_max)


def _diff_attn(q_proj, kv_projs, lam_params, da_g, *, lam_init):
    B, Sq, _ = q_proj.shape
    tq = min(DA_TILES_PER_STEP * DA_Q_TILE, Sq)
    in_specs = [
        pl.BlockSpec((None, tq, HEAD_DIM), lambda b, h, i: (b, i, COL_AQ + h)),
        pl.BlockSpec((None, tq, HEAD_DIM), lambda b, h, i: (b, i, COL_AZ + h)),
    ]
    args = [q_proj, q_proj]
    for kp in kv_projs:
        sk = kp.shape[1]
        in_specs.append(pl.BlockSpec((None, sk, HEAD_DIM), lambda b, h, i: (b, 0, COL_AK + h)))
        in_specs.append(pl.BlockSpec((None, sk, HEAD_DIM), lambda b, h, i: (b, 0, COL_AV + h)))
        args += [kp, kp]
    in_specs += [
        pl.BlockSpec((4, DA_QK_DIM), lambda b, h, i: (0, 0)),
        pl.BlockSpec((1, HEAD_DIM), lambda b, h, i: (0, 0)),
    ]
    args += [lam_params, da_g.reshape(1, HEAD_DIM)]
    return pl.pallas_call(
        functools.partial(_da_kernel, n_src=len(kv_projs), lam_init=lam_init),
        grid=(B, DA_HEADS, Sq // tq),
        in_specs=in_specs,
        out_specs=pl.BlockSpec((None, tq, HEAD_DIM), lambda b, h, i: (b, i, h)),
        out_shape=jax.ShapeDtypeStruct((B, Sq, DA_WIDTH), _BF16),
        scratch_shapes=[pltpu.VMEM((2, 2 * DA_Q_TILE, sum(kp.shape[1] for kp in kv_projs)), _F32)],
        name="diff_attn",
        compiler_params=_params("parallel", "parallel", "parallel"),
    )(*args)


def _na_kernel(q_ref, z_ref, k_ref, v_ref, kc_ref, vc_ref, *rest):
    bias_refs, (g_ref, o_ref) = rest[:NA_BLOCKS_PER_STEP], rest[NA_BLOCKS_PER_STEP:]
    step = pl.program_id(1)
    rows = k_ref.shape[0] // GRID_W
    n_ctx = kc_ref.shape[0]
    ones = jnp.ones((NA_K, HEAD_DIM), _BF16)

    def key_start(j):
        rb = step * NA_BLOCKS_PER_STEP + j
        kr0 = jnp.clip(rb * NA_BLOCK_ROWS - NA_WIN_H // 2, 0, rows - NA_KEY_ROWS)
        return pl.multiple_of(kr0 * GRID_W, GRID_W)

    def scores_and_max(j, h):
        cols = slice(h * HEAD_DIM, (h + 1) * HEAD_DIM)
        q = q_ref[j * NA_Q:(j + 1) * NA_Q, cols]
        s_loc = _dot_nt(q, k_ref[pl.ds(key_start(j), NA_K), cols]) + bias_refs[j][h]
        s_ctx = _dot_nt(q, kc_ref[:, cols])
        m = jnp.maximum(jnp.max(s_loc, axis=-1, keepdims=True), jnp.max(s_ctx, axis=-1, keepdims=True))
        return s_loc, s_ctx, m

    def softmax_pv_out(j, h, s_loc, s_ctx, m):
        cols = slice(h * HEAD_DIM, (h + 1) * HEAD_DIM)
        v_loc = jnp.concatenate([v_ref[pl.ds(key_start(j), NA_K), cols], ones], axis=1)
        v_ctx = jnp.concatenate([vc_ref[:, cols], ones[:n_ctx]], axis=1)
        acc = jnp.dot(jnp.exp2(s_loc - m).astype(_BF16), v_loc, preferred_element_type=_F32)
        acc = acc + jnp.dot(jnp.exp2(s_ctx - m).astype(_BF16), v_ctx, preferred_element_type=_F32)
        o = acc[:, :HEAD_DIM] / acc[:, HEAD_DIM:]
        qrows = slice(j * NA_Q, (j + 1) * NA_Q)
        z = z_ref[qrows, cols].astype(_F32)
        o_ref[qrows, cols] = (_rms(o, g_ref[:, cols]) * _silu(z)).astype(_BF16)

    units = [(j, h) for j in range(NA_BLOCKS_PER_STEP) for h in range(NA_HEADS)]
    pending = scores_and_max(*units[0])
    for u in range(1, len(units)):
        nxt = scores_and_max(*units[u])
        softmax_pv_out(*units[u - 1], *pending)
        pending = nxt
    softmax_pv_out(*units[-1], *pending)


def _na_bias_kernel(r_ref, o_ref, *, rows):
    shape = (GRID_W, 2 * GRID_W)
    c = lax.broadcasted_iota(jnp.int32, shape, 0)
    lane = lax.broadcasted_iota(jnp.int32, shape, 1)
    kc = lane & (GRID_W - 1)
    cs = jnp.clip(c - NA_WIN_W // 2, 0, GRID_W - NA_WIN_W)
    col_valid = (kc >= cs) & (kc < cs + NA_WIN_W)
    first_key_row = lane < GRID_W
    toeplitz = []
    for t in range(2 * NA_WIN_H):
        row = jnp.broadcast_to(r_ref[t:t + 1, :], shape)
        x = pltpu.roll(row, 2 * GRID_W - (NA_WIN_W - 1), 1, stride=1, stride_axis=0)
        toeplitz.append(jnp.where(col_valid, x * LOG2E, MASK_VALUE))
    masked = jnp.full(shape, MASK_VALUE, _F32)
    for cfg, (r0, kr0) in enumerate(_na_block_configs(rows)):
        for ri in range(NA_BLOCK_ROWS):
            r = r0 + ri
            rs = min(max(r - NA_WIN_H // 2, 0), rows - NA_WIN_H)
            for j in range(NA_KEY_ROWS // 2):
                kra = kr0 + 2 * j
                valid_a = rs <= kra < rs + NA_WIN_H
                valid_b = rs <= kra + 1 < rs + NA_WIN_H
                t = kra - r + NA_WIN_H
                if valid_a and valid_b:
                    blk = toeplitz[t]
                elif valid_a:
                    blk = jnp.where(first_key_row, toeplitz[t], MASK_VALUE)
                elif valid_b:
                    blk = jnp.where(first_key_row, MASK_VALUE, toeplitz[t])
                else:
                    blk = masked
                o_ref[cfg, ri * GRID_W:(ri + 1) * GRID_W, j * 2 * GRID_W:(j + 1) * 2 * GRID_W] = blk


def _na_block_configs(rows):
    return ((0, 0), (2 * NA_BLOCK_ROWS, 2 * NA_BLOCK_ROWS - NA_WIN_H // 2),
            (rows - NA_BLOCK_ROWS, rows - NA_KEY_ROWS))


def _na_bias_tables(rpb, rows):
    H, nr, ncol = rpb.shape
    lo = jnp.pad(rpb, ((0, 0), (1, 0), (0, 2 * GRID_W - ncol)))
    hi = jnp.pad(rpb, ((0, 0), (0, 1), (GRID_W, GRID_W - ncol)))
    return pl.pallas_call(
        functools.partial(_na_bias_kernel, rows=rows),
        grid=(H,),
        in_specs=[pl.BlockSpec((None, nr + 1, 2 * GRID_W), lambda h: (h, 0, 0))],
        out_specs=pl.BlockSpec((3, None, NA_Q, NA_K), lambda h: (0, h, 0, 0)),
        out_shape=jax.ShapeDtypeStruct((3, H, NA_Q, NA_K), _F32),
        name="na_bias",
        compiler_params=_params("parallel"),
    )(lo + hi)


def _nbr_attn(proj, proj_ctx, bias_tabs, na_g):
    B, S, _ = proj.shape
    C = proj_ctx.shape[1]
    n_blocks = S // NA_Q
    step_q = NA_BLOCKS_PER_STEP * NA_Q

    def bias_spec(j):
        def idx(b, i):
            blk = i * NA_BLOCKS_PER_STEP + j
            return (jnp.where(blk == 0, 0, jnp.where(blk == n_blocks - 1, 2, 1)), 0, 0, 0)
        return pl.BlockSpec((None, NA_HEADS, NA_Q, NA_K), idx)

    cq, ck, cv, cz = (c * HEAD_DIM // NA_WIDTH for c in (COL_BQ, COL_BK, COL_BV, COL_BZ))
    return pl.pallas_call(
        _na_kernel,
        grid=(B, n_blocks // NA_BLOCKS_PER_STEP),
        in_specs=[
            pl.BlockSpec((None, step_q, NA_WIDTH), lambda b, i: (b, i, cq)),
            pl.BlockSpec((None, step_q, NA_WIDTH), lambda b, i: (b, i, cz)),
            pl.BlockSpec((None, S, NA_WIDTH), lambda b, i: (b, 0, ck), pipeline_mode=pl.Buffered(1)),
            pl.BlockSpec((None, S, NA_WIDTH), lambda b, i: (b, 0, cv), pipeline_mode=pl.Buffered(1)),
            pl.BlockSpec((None, C, NA_WIDTH), lambda b, i: (b, 0, ck)),
            pl.BlockSpec((None, C, NA_WIDTH), lambda b, i: (b, 0, cv)),
            *[bias_spec(j) for j in range(NA_BLOCKS_PER_STEP)],
            pl.BlockSpec((1, NA_WIDTH), lambda b, i: (0, 0)),
        ],
        out_specs=pl.BlockSpec((None, step_q, NA_WIDTH), lambda b, i: (b, i, 0)),
        out_shape=jax.ShapeDtypeStruct((B, S, NA_WIDTH), _BF16),
        name="nbr_attn",
        compiler_params=_params("parallel", "parallel"),
    )(proj, proj, proj, proj, proj_ctx, proj_ctx, *([bias_tabs] * NA_BLOCKS_PER_STEP), na_g.reshape(1, NA_WIDTH))


def _ctx_attn_kernel(q_ref, z_ref, k_ref, v_ref, g_ref, o_ref):
    o = _softmax_pv(q_ref[...], [(k_ref[...], v_ref[...], None)])
    _branch_out(o, g_ref[...], 1.0, z_ref, o_ref)


def _ctx_attn(proj_ctx, na_g):
    B, C, _ = proj_ctx.shape
    return pl.pallas_call(
        _ctx_attn_kernel,
        grid=(B, NA_HEADS),
        in_specs=[
            pl.BlockSpec((None, C, HEAD_DIM), lambda b, h: (b, 0, COL_BQ + h)),
            pl.BlockSpec((None, C, HEAD_DIM), lambda b, h: (b, 0, COL_BZ + h)),
            pl.BlockSpec((None, C, HEAD_DIM), lambda b, h: (b, 0, COL_BK + h)),
            pl.BlockSpec((None, C, HEAD_DIM), lambda b, h: (b, 0, COL_BV + h)),
            pl.BlockSpec((None, 1, HEAD_DIM), lambda b, h: (h, 0, 0)),
        ],
        out_specs=pl.BlockSpec((None, C, HEAD_DIM), lambda b, h: (b, 0, h)),
        out_shape=jax.ShapeDtypeStruct((B, C, NA_WIDTH), _BF16),
        name="ctx_attn",
        compiler_params=_params("parallel", "parallel"),
    )(proj_ctx, proj_ctx, proj_ctx, proj_ctx, na_g.reshape(NA_HEADS, 1, HEAD_DIM))


def _gmlp_rows(rows, u_ref, v_ref, z_ref, lng_ref, lnb_ref, ws_ref, bs_ref, g_ref):
    n_chunks = (rows.stop - rows.start) // CHUNK
    groups = []
    for g in range(MLP_GROUPS):
        cols = slice(g * HEAD_DIM, (g + 1) * HEAD_DIM)
        w = ws_ref[g].astype(_BF16)
        bs = jnp.concatenate([bs_ref[g]] * n_chunks, axis=0)
        v = v_ref[rows, cols].astype(_F32)
        vc = v - jnp.mean(v, axis=-1, keepdims=True)
        vn = vc * lax.rsqrt(jnp.mean(vc * vc, axis=-1, keepdims=True) + EPS) * lng_ref[:, cols] + lnb_ref[:, cols]
        vn = vn.astype(_BF16)
        mixed = jnp.concatenate(
            [jnp.dot(w, vn[c * CHUNK:(c + 1) * CHUNK, :], preferred_element_type=_F32) for c in range(n_chunks)],
            axis=0) + bs
        om = u_ref[rows, cols].astype(_F32) * mixed
        z = z_ref[rows, cols].astype(_F32)
        groups.append((_rms(om, g_ref[:, cols]) * _silu(z)).astype(_BF16))
    return jnp.concatenate(groups, axis=1)


def _out_kernel(ya_ref, yn_ref, u_ref, v_ref, z_ref, lng_ref, lnb_ref, ws_ref, bs_ref, mg_ref,
                w_ref, x_ref, gate_ref, g_ref, o_ref):
    sub_tiles = [slice(r0, r0 + OUT_SUB_ROWS) for r0 in range(0, x_ref.shape[0], OUT_SUB_ROWS)]
    yms = [_gmlp_rows(rows, u_ref, v_ref, z_ref, lng_ref, lnb_ref, ws_ref, bs_ref, mg_ref) for rows in sub_tiles]
    for rows, ym in zip(sub_tiles, yms):
        y = jnp.dot(ya_ref[rows, :], w_ref[:DA_WIDTH, :], preferred_element_type=_F32)
        y = y + jnp.dot(yn_ref[rows, :], w_ref[DA_WIDTH:DA_WIDTH + NA_WIDTH, :], preferred_element_type=_F32)
        y = y + jnp.dot(ym, w_ref[DA_WIDTH + NA_WIDTH:, :], preferred_element_type=_F32)
        o_ref[rows, :] = x_ref[rows, :] + gate_ref[...] * _rms(y, g_ref[...])


def _out_proj(ya, yn, proj, mlp_params, w_bf16, layer, x2d, gate, post_g):
    T, D = x2d.shape
    nb = gate.shape[0]
    tm = min(OUT_TILE_ROWS, T // nb)
    tiles_per_mod = (T // nb) // tm
    ln_g, ln_b, ws, bs, out_g = mlp_params
    proj2d = proj.reshape(T, IN_WIDTH)
    vec = pl.BlockSpec((1, MLP_WIDTH), lambda i: (0, 0))
    return pl.pallas_call(
        _out_kernel,
        grid=(T // tm,),
        in_specs=[
            pl.BlockSpec((tm, DA_WIDTH), lambda i: (i, 0)),
            pl.BlockSpec((tm, NA_WIDTH), lambda i: (i, 0)),
            pl.BlockSpec((tm, MLP_WIDTH), lambda i: (i, COL_CU)),
            pl.BlockSpec((tm, MLP_WIDTH), lambda i: (i, COL_CV)),
            pl.BlockSpec((tm, MLP_WIDTH), lambda i: (i, COL_CZ)),
            vec,
            vec,
            pl.BlockSpec((MLP_GROUPS, CHUNK, CHUNK), lambda i: (0, 0, 0)),
            pl.BlockSpec((MLP_GROUPS, CHUNK, 1), lambda i: (0, 0, 0)),
            vec,
            pl.BlockSpec((None, MIX_WIDTH, D), lambda i: (layer, 0, 0), pipeline_mode=pl.Buffered(1)),
            pl.BlockSpec((tm, D), lambda i: (i, 0)),
            pl.BlockSpec((None, 1, D), lambda i: (i // tiles_per_mod, 0, 0)),
            pl.BlockSpec((1, D), lambda i: (0, 0)),
        ],
        out_specs=pl.BlockSpec((tm, D), lambda i: (i, 0)),
        out_shape=jax.ShapeDtypeStruct((T, D), _F32),
        name="out_proj",
        compiler_params=_params("parallel"),
    )(ya.reshape(T, DA_WIDTH), yn.reshape(T, NA_WIDTH), proj2d, proj2d, proj2d,
      ln_g.reshape(1, MLP_WIDTH), ln_b.reshape(1, MLP_WIDTH), ws, bs.reshape(MLP_GROUPS, CHUNK, 1),
      out_g.reshape(1, MLP_WIDTH), w_bf16, x2d, gate, post_g.reshape(1, D))


def _rope_tables(seq):
    n = DA_QK_DIM // 4
    freqs = ROPE_BASE ** (-np.arange(n, dtype=np.float64) / n)
    pos = np.arange(seq)
    row = (pos // GRID_W).astype(np.float64)[:, None] * freqs
    col = (pos % GRID_W).astype(np.float64)[:, None] * freqs
    zeros = np.zeros((seq, n))

    def one_map(fn_first, fn_second):
        return np.concatenate([fn_first(row), fn_second(row), fn_first(col), fn_second(col)], axis=-1)

    cos = one_map(np.cos, np.cos)
    sa = one_map(lambda a: -np.sin(a), lambda a: zeros)
    sb = one_map(lambda a: zeros, np.sin)
    return tuple(jnp.asarray(np.concatenate([t, t], axis=-1), _F32) for t in (cos, sa, sb))


def _identity_rope_tables(seq):
    return (jnp.ones((seq, HEAD_DIM), _F32), jnp.zeros((seq, HEAD_DIM), _F32),
            jnp.zeros((seq, HEAD_DIM), _F32))


def kernel(x, c, ctx, c_ctx, ada_w, ada_b, pre_g, post_g, w_in, w_out, lam_q1, lam_k1, lam_q2, lam_k2,
           da_g, na_rpb, na_g, mlp_ln_g, mlp_ln_b, mlp_ws, mlp_bs, mlp_g):
    B, S, D = x.shape
    C = ctx.shape[1]
    L = ada_w.shape[0]
    assert S % (NA_BLOCKS_PER_STEP * NA_Q) == 0 and S // GRID_W >= NA_KEY_ROWS and C % CHUNK == 0

    cc = jnp.concatenate([c, c_ctx[None], jnp.zeros((8 - B - 1, D), _F32)], axis=0)
    mod = _ada(cc, ada_w, ada_b)

    rope = _rope_tables(S)
    rope_id = _identity_rope_tables(C)
    w_in_b = w_in.astype(_BF16)
    w_out_b = w_out.astype(_BF16)
    x2d = x.reshape(B * S, D)
    xc2d = ctx.reshape(B * C, D)

    for l in range(L):
        lam_init = 0.8 - 0.6 * math.exp(-0.3 * l)
        ctx_out = l < L - 1
        shift, scale, gate = (mod[l, :B, i * D:(i + 1) * D].reshape(B, 1, D) for i in range(3))
        shift_c, scale_c, gate_c = (mod[l, B:B + 1, i * D:(i + 1) * D].reshape(1, 1, D) for i in range(3))
        lam_params = jnp.stack([lam_q1[l], lam_k1[l], lam_q2[l], lam_k2[l]])

        proj = _in_proj(x2d, pre_g[l], scale, shift, w_in_b, l, *rope, seq=S).reshape(B, S, IN_WIDTH)
        proj_c = _in_proj(xc2d, pre_g[l], scale_c, shift_c, w_in_b, l, *rope_id, seq=C,
                          kv_only=not ctx_out).reshape(B, C, IN_WIDTH)

        ya = _diff_attn(proj, [proj, proj_c], lam_params, da_g[l], lam_init=lam_init)
        yn = _nbr_attn(proj, proj_c, _na_bias_tables(na_rpb[l], S // GRID_W), na_g[l])
        mlp_params = (mlp_ln_g[l], mlp_ln_b[l], mlp_ws[l], mlp_bs[l], mlp_g[l])
        x2d_new = _out_proj(ya, yn, proj, mlp_params, w_out_b, l, x2d, gate, post_g[l])

        if ctx_out:
            yca = _diff_attn(proj_c, [proj_c], lam_params, da_g[l], lam_init=lam_init)
            ycn = _ctx_attn(proj_c, na_g[l])
            xc2d = _out_proj(yca, ycn, proj_c, mlp_params, w_out_b, l, xc2d, gate_c, post_g[l])
        x2d = x2d_new

    return x2d.reshape(B, S, D)
```

```python
import functools
import math

import numpy as np
import jax
import jax.numpy as jnp
from jax import lax
from jax.experimental import pallas as pl
from jax.experimental.pallas import tpu as pltpu

D_MODEL = 2048
DEPTH = 2
GRID_W = 64
HEAD_DIM = 128
DA_HEADS = 6
NA_HEADS = 6
MLP_GROUPS = 4
DA_WIDTH = DA_HEADS * HEAD_DIM
NA_WIDTH = NA_HEADS * HEAD_DIM
MLP_WIDTH = MLP_GROUPS * HEAD_DIM
DA_QK_DIM = HEAD_DIM // 2
NA_WIN_H = 8
NA_WIN_W = 16
CHUNK = 128
ROPE_BASE = 10000.0
EPS = 1e-6
IN_WIDTH = 4 * DA_WIDTH + 4 * NA_WIDTH + 3 * MLP_WIDTH
MIX_WIDTH = DA_WIDTH + NA_WIDTH + MLP_WIDTH

COL_AQ, COL_AK, COL_AV, COL_AZ = 0, 6, 12, 18
COL_BQ, COL_BK, COL_BV, COL_BZ = 24, 30, 36, 42
COL_CU, COL_CV, COL_CZ = 12, 13, 14

LOG2E = math.log2(math.e)
DA_SCALE = DA_QK_DIM ** -0.5 * LOG2E
NA_SCALE = HEAD_DIM ** -0.5 * LOG2E
DA_KEY_CHUNK = 512
DA_Q_TILE = 128
DA_TILES_PER_STEP = 16


def _in_segments():
    kinds = ["rope_scale", "rope", "plain", "plain", "na_scale", "plain", "plain", "plain", "gelu", "gelu", "plain"]
    widths = [DA_WIDTH] * 4 + [NA_WIDTH] * 4 + [MLP_WIDTH] * 3
    segs, c0 = [], 0
    for width, kind in zip(widths, kinds):
        segs.append((c0, width, kind))
        c0 += width
    return tuple(segs)


IN_SEGMENTS = _in_segments()
IN_KV_SEGMENTS = (1, 2, 5, 6)
IN_TILE_ROWS = 256
OUT_TILE_ROWS = 512
OUT_SUB_ROWS = 256

NA_BLOCK_ROWS = 4
NA_KEY_ROWS = NA_BLOCK_ROWS + NA_WIN_H
NA_BLOCKS_PER_STEP = 2
NA_Q = NA_BLOCK_ROWS * GRID_W
NA_K = NA_KEY_ROWS * GRID_W
MASK_VALUE = -1e30

VMEM_LIMIT_BYTES = 56 * 1024 * 1024

_BF16 = jnp.bfloat16
_F32 = jnp.float32


def _silu(x):
    return x * (1.0 / (1.0 + jnp.exp(-x)))


def _gelu_tanh(x):
    c = math.sqrt(2.0 / math.pi)
    return 0.5 * x * (1.0 + jnp.tanh(c * (x + 0.044715 * (x * x * x))))


def _rms(x, g):
    return x * lax.rsqrt(jnp.mean(x * x, axis=-1, keepdims=True) + EPS) * g


def _dot_nt(a, b):
    return lax.dot_general(a, b, (((1,), (1,)), ((), ())), preferred_element_type=_F32)


def _params(*sem):
    return pltpu.CompilerParams(dimension_semantics=sem, vmem_limit_bytes=VMEM_LIMIT_BYTES)


def _ada_kernel(c_ref, w_ref, b_ref, o_ref):
    sc = _silu(c_ref[...]).astype(_BF16)
    o_ref[...] = jnp.dot(sc, w_ref[...].astype(_BF16), preferred_element_type=_F32) + b_ref[...]


def _ada(cc, ada_w, ada_b):
    L, D, N = ada_w.shape
    R = cc.shape[0]
    tn = 768
    return pl.pallas_call(
        _ada_kernel,
        grid=(L, N // tn),
        in_specs=[
            pl.BlockSpec((R, D), lambda l, j: (0, 0)),
            pl.BlockSpec((None, D, tn), lambda l, j: (l, 0, j)),
            pl.BlockSpec((None, 1, tn), lambda l, j: (l, 0, j)),
        ],
        out_specs=pl.BlockSpec((None, R, tn), lambda l, j: (l, 0, j)),
        out_shape=jax.ShapeDtypeStruct((L, R, N), _F32),
        name="ada_mod",
        compiler_params=_params("parallel", "parallel"),
    )(cc, ada_w, ada_b.reshape(L, 1, N))


def _in_kernel(x_ref, g_ref, scale_ref, shift_ref, w_ref, cos_ref, sa_ref, sb_ref, o_ref, *, kv_only):
    x = x_ref[...]
    h = (_rms(x, g_ref[...]) * (1.0 + scale_ref[...]) + shift_ref[...]).astype(_BF16)
    cos, sa, sb = cos_ref[...], sa_ref[...], sb_ref[...]
    for idx, (c0, width, epilogue) in enumerate(IN_SEGMENTS):
        if kv_only and idx not in IN_KV_SEGMENTS:
            o_ref[:, c0:c0 + width] = jnp.zeros((o_ref.shape[0], width), _BF16)
            continue
        acc = jnp.dot(h, w_ref[:, c0:c0 + width], preferred_element_type=_F32)
        if epilogue in ("rope", "rope_scale"):
            for hh in range(width // HEAD_DIM):
                xs = acc[:, hh * HEAD_DIM:(hh + 1) * HEAD_DIM]
                r = xs * cos + pltpu.roll(xs, HEAD_DIM - 16, 1) * sa + pltpu.roll(xs, 16, 1) * sb
                if epilogue == "rope_scale":
                    r = r * DA_SCALE
                o_ref[:, c0 + hh * HEAD_DIM:c0 + (hh + 1) * HEAD_DIM] = r.astype(_BF16)
        elif epilogue == "na_scale":
            o_ref[:, c0:c0 + width] = (acc * NA_SCALE).astype(_BF16)
        elif epilogue == "gelu":
            o_ref[:, c0:c0 + width] = _gelu_tanh(acc).astype(_BF16)
        else:
            o_ref[:, c0:c0 + width] = acc.astype(_BF16)


def _in_proj(x2d, pre_g, scale, shift, w_bf16, layer, cos, sa, sb, *, seq, kv_only=False):
    T, D = x2d.shape
    nb = scale.shape[0]
    tm = IN_TILE_ROWS
    tiles_per_mod = (T // nb) // tm
    tiles_per_seq = seq // tm
    return pl.pallas_call(
        functools.partial(_in_kernel, kv_only=kv_only),
        grid=(T // tm,),
        in_specs=[
            pl.BlockSpec((tm, D), lambda i: (i, 0)),
            pl.BlockSpec((1, D), lambda i: (0, 0)),
            pl.BlockSpec((None, 1, D), lambda i: (i // tiles_per_mod, 0, 0)),
            pl.BlockSpec((None, 1, D), lambda i: (i // tiles_per_mod, 0, 0)),
            pl.BlockSpec((None, D, IN_WIDTH), lambda i: (layer, 0, 0), pipeline_mode=pl.Buffered(1)),
            pl.BlockSpec((tm, HEAD_DIM), lambda i: (i % tiles_per_seq, 0)),
            pl.BlockSpec((tm, HEAD_DIM), lambda i: (i % tiles_per_seq, 0)),
            pl.BlockSpec((tm, HEAD_DIM), lambda i: (i % tiles_per_seq, 0)),
        ],
        out_specs=pl.BlockSpec((tm, IN_WIDTH), lambda i: (i, 0)),
        out_shape=jax.ShapeDtypeStruct((T, IN_WIDTH), _BF16),
        name="in_proj",
        compiler_params=_params("parallel"),
    )(x2d, pre_g.reshape(1, D), scale, shift, w_bf16, cos, sa, sb)


def _softmax_pv(qq, sources):
    scores = []
    for k, _, bias in sources:
        s = _dot_nt(qq, k)
        if bias is not None:
            s = s + bias
        scores.append(s)
    m = jnp.max(scores[0], axis=-1, keepdims=True)
    for s in scores[1:]:
        m = jnp.maximum(m, jnp.max(s, axis=-1, keepdims=True))
    acc = None
    for s, (_, v, _) in zip(scores, sources):
        p = jnp.exp2(s - m).astype(_BF16)
        v_aug = jnp.concatenate([v, jnp.ones(v.shape, _BF16)], axis=1)
        pv = jnp.dot(p, v_aug, preferred_element_type=_F32)
        acc = pv if acc is None else acc + pv
    return acc[:, :HEAD_DIM] / acc[:, HEAD_DIM:]


def _branch_out(o, g, factor, z_ref, o_ref):
    z = z_ref[...].astype(_F32)
    o_ref[...] = (_rms(o, g) * factor * _silu(z)).astype(_BF16)


def _da_kernel(*refs, n_src, lam_init):
    q_ref, z_ref = refs[0], refs[1]
    kv_refs = refs[2:2 + 2 * n_src]
    lam_ref, g_ref, o_ref, s_ref = refs[2 + 2 * n_src:]
    tq = DA_Q_TILE
    n_tiles = q_ref.shape[0] // tq

    chunks = []
    col = 0
    for i in range(n_src):
        k_ref, v_ref = kv_refs[2 * i], kv_refs[2 * i + 1]
        size = k_ref.shape[0]
        ck = min(size, DA_KEY_CHUNK)
        for c in range(size // ck):
            chunks.append((k_ref, v_ref, c * ck, ck, col))
            col += ck

    lp = lam_ref[...]
    lam = (jnp.exp(jnp.sum(lp[0:1] * lp[1:2], axis=-1, keepdims=True))
           - jnp.exp(jnp.sum(lp[2:3] * lp[3:4], axis=-1, keepdims=True)) + lam_init)

    def scores_and_max(t):
        q = q_ref[t * tq:(t + 1) * tq, :]
        lane = lax.broadcasted_iota(jnp.int32, q.shape, 1)
        zero = jnp.zeros_like(q)
        qq = jnp.concatenate([jnp.where(lane < DA_QK_DIM, q, zero),
                              jnp.where(lane >= DA_QK_DIM, q, zero)], axis=0)
        mx = None
        for k_ref, _, r0, ck, c0 in chunks:
            s = _dot_nt(qq, k_ref[r0:r0 + ck, :])
            s_ref[t % 2, :, c0:c0 + ck] = s
            for jj in range(ck // HEAD_DIM):
                part = s[:, jj * HEAD_DIM:(jj + 1) * HEAD_DIM]
                mx = part if mx is None else jnp.maximum(mx, part)
        return jnp.max(mx, axis=-1, keepdims=True)

    def softmax_pv_out(t, m):
        acc = None
        for _, v_ref, r0, ck, c0 in chunks:
            p = jnp.exp2(s_ref[t % 2, :, c0:c0 + ck] - m).astype(_BF16)
            v_aug = jnp.concatenate([v_ref[r0:r0 + ck, :], jnp.ones((ck, HEAD_DIM), _BF16)], axis=1)
            pv = jnp.dot(p, v_aug, preferred_element_type=_F32)
            acc = pv if acc is None else acc + pv
        o = acc[:, :HEAD_DIM] / acc[:, HEAD_DIM:]
        oa = o[:tq] - lam * o[tq:]
        rows = slice(t * tq, (t + 1) * tq)
        z = z_ref[rows, :].astype(_F32)
        o_ref[rows, :] = (_rms(oa, g_ref[...]) * (1.0 - lam_init) * _silu(z)).astype(_BF16)

    row_max = scores_and_max(0)
    for t in range(1, n_tiles):
        next_max = scores_and_max(t)
        softmax_pv_out(t - 1, row_max)
        row_max = next_max
    softmax_pv_out(n_tiles - 1, row_max)


def _diff_attn(q_proj, kv_projs, lam_params, da_g, *, lam_init):
    B, Sq, _ = q_proj.shape
    tq = min(DA_TILES_PER_STEP * DA_Q_TILE, Sq)
    in_specs = [
        pl.BlockSpec((None, tq, HEAD_DIM), lambda b, h, i: (b, i, COL_AQ + h)),
        pl.BlockSpec((None, tq, HEAD_DIM), lambda b, h, i: (b, i, COL_AZ + h)),
    ]
    args = [q_proj, q_proj]
    for kp in kv_projs:
        sk = kp.shape[1]
        in_specs.append(pl.BlockSpec((None, sk, HEAD_DIM), lambda b, h, i: (b, 0, COL_AK + h)))
        in_specs.append(pl.BlockSpec((None, sk, HEAD_DIM), lambda b, h, i: (b, 0, COL_AV + h)))
        args += [kp, kp]
    in_specs += [
        pl.BlockSpec((4, DA_QK_DIM), lambda b, h, i: (0, 0)),
        pl.BlockSpec((1, HEAD_DIM), lambda b, h, i: (0, 0)),
    ]
    args += [lam_params, da_g.reshape(1, HEAD_DIM)]
    return pl.pallas_call(
        functools.partial(_da_kernel, n_src=len(kv_projs), lam_init=lam_init),
        grid=(B, DA_HEADS, Sq // tq),
        in_specs=in_specs,
        out_specs=pl.BlockSpec((None, tq, HEAD_DIM), lambda b, h, i: (b, i, h)),
        out_shape=jax.ShapeDtypeStruct((B, Sq, DA_WIDTH), _BF16),
        scratch_shapes=[pltpu.VMEM((2, 2 * DA_Q_TILE, sum(kp.shape[1] for kp in kv_projs)), _F32)],
        name="diff_attn",
        compiler_params=_params("parallel", "parallel", "parallel"),
    )(*args)


def _na_kernel(q_ref, z_ref, k_ref, v_ref, kc_ref, vc_ref, bias_ref, g_ref, o_ref):
    step = pl.program_id(1)
    rows = k_ref.shape[0] // GRID_W
    n_blocks = rows // NA_BLOCK_ROWS
    n_ctx = kc_ref.shape[0]
    ones = jnp.ones((NA_K, HEAD_DIM), _BF16)

    def key_start(j):
        rb = step * NA_BLOCKS_PER_STEP + j
        kr0 = jnp.clip(rb * NA_BLOCK_ROWS - NA_WIN_H // 2, 0, rows - NA_KEY_ROWS)
        return pl.multiple_of(kr0 * GRID_W, GRID_W)

    def bias_table(j):
        rb = step * NA_BLOCKS_PER_STEP + j
        return jnp.where(rb == 0, 0, jnp.where(rb == n_blocks - 1, 2, 1))

    def scores_and_max(j, h):
        cols = slice(h * HEAD_DIM, (h + 1) * HEAD_DIM)
        q = q_ref[j * NA_Q:(j + 1) * NA_Q, cols]
        s_loc = _dot_nt(q, k_ref[pl.ds(key_start(j), NA_K), cols]) + bias_ref[bias_table(j), h]
        s_ctx = _dot_nt(q, kc_ref[:, cols])
        m = jnp.maximum(jnp.max(s_loc, axis=-1, keepdims=True), jnp.max(s_ctx, axis=-1, keepdims=True))
        return s_loc, s_ctx, m

    def softmax_pv_out(j, h, s_loc, s_ctx, m):
        cols = slice(h * HEAD_DIM, (h + 1) * HEAD_DIM)
        v_loc = jnp.concatenate([v_ref[pl.ds(key_start(j), NA_K), cols], ones], axis=1)
        v_ctx = jnp.concatenate([vc_ref[:, cols], ones[:n_ctx]], axis=1)
        acc = jnp.dot(jnp.exp2(s_loc - m).astype(_BF16), v_loc, preferred_element_type=_F32)
        acc = acc + jnp.dot(jnp.exp2(s_ctx - m).astype(_BF16), v_ctx, preferred_element_type=_F32)
        o = acc[:, :HEAD_DIM] / acc[:, HEAD_DIM:]
        qrows = slice(j * NA_Q, (j + 1) * NA_Q)
        z = z_ref[qrows, cols].astype(_F32)
        o_ref[qrows, cols] = (_rms(o, g_ref[:, cols]) * _silu(z)).astype(_BF16)

    units = [(j, h) for j in range(NA_BLOCKS_PER_STEP) for h in range(NA_HEADS)]
    pending = scores_and_max(*units[0])
    for u in range(1, len(units)):
        nxt = scores_and_max(*units[u])
        softmax_pv_out(*units[u - 1], *pending)
        pending = nxt
    softmax_pv_out(*units[-1], *pending)


def _na_bias_kernel(r_ref, o_ref, *, rows):
    shape = (GRID_W, 2 * GRID_W)
    c = lax.broadcasted_iota(jnp.int32, shape, 0)
    lane = lax.broadcasted_iota(jnp.int32, shape, 1)
    kc = lane & (GRID_W - 1)
    cs = jnp.clip(c - NA_WIN_W // 2, 0, GRID_W - NA_WIN_W)
    col_valid = (kc >= cs) & (kc < cs + NA_WIN_W)
    first_key_row = lane < GRID_W
    toeplitz = []
    for t in range(2 * NA_WIN_H):
        row = jnp.broadcast_to(r_ref[t:t + 1, :], shape)
        x = pltpu.roll(row, 2 * GRID_W - (NA_WIN_W - 1), 1, stride=1, stride_axis=0)
        toeplitz.append(jnp.where(col_valid, x * LOG2E, MASK_VALUE))
    masked = jnp.full(shape, MASK_VALUE, _F32)
    for cfg, (r0, kr0) in enumerate(_na_block_configs(rows)):
        for ri in range(NA_BLOCK_ROWS):
            r = r0 + ri
            rs = min(max(r - NA_WIN_H // 2, 0), rows - NA_WIN_H)
            for j in range(NA_KEY_ROWS // 2):
                kra = kr0 + 2 * j
                valid_a = rs <= kra < rs + NA_WIN_H
                valid_b = rs <= kra + 1 < rs + NA_WIN_H
                t = kra - r + NA_WIN_H
                if valid_a and valid_b:
                    blk = toeplitz[t]
                elif valid_a:
                    blk = jnp.where(first_key_row, toeplitz[t], MASK_VALUE)
                elif valid_b:
                    blk = jnp.where(first_key_row, MASK_VALUE, toeplitz[t])
                else:
                    blk = masked
                o_ref[cfg, ri * GRID_W:(ri + 1) * GRID_W, j * 2 * GRID_W:(j + 1) * 2 * GRID_W] = blk


def _na_block_configs(rows):
    return ((0, 0), (2 * NA_BLOCK_ROWS, 2 * NA_BLOCK_ROWS - NA_WIN_H // 2),
            (rows - NA_BLOCK_ROWS, rows - NA_KEY_ROWS))


def _na_bias_tables(rpb, rows):
    H, nr, ncol = rpb.shape
    lo = jnp.pad(rpb, ((0, 0), (1, 0), (0, 2 * GRID_W - ncol)))
    hi = jnp.pad(rpb, ((0, 0), (0, 1), (GRID_W, GRID_W - ncol)))
    return pl.pallas_call(
        functools.partial(_na_bias_kernel, rows=rows),
        grid=(H,),
        in_specs=[pl.BlockSpec((None, nr + 1, 2 * GRID_W), lambda h: (h, 0, 0))],
        out_specs=pl.BlockSpec((3, None, NA_Q, NA_K), lambda h: (0, h, 0, 0)),
        out_shape=jax.ShapeDtypeStruct((3, H, NA_Q, NA_K), _F32),
        name="na_bias",
        compiler_params=_params("parallel"),
    )(lo + hi)


def _nbr_attn(proj, proj_ctx, bias_tabs, na_g):
    B, S, _ = proj.shape
    C = proj_ctx.shape[1]
    n_blocks = S // NA_Q
    step_q = NA_BLOCKS_PER_STEP * NA_Q

    cq, ck, cv, cz = (c * HEAD_DIM // NA_WIDTH for c in (COL_BQ, COL_BK, COL_BV, COL_BZ))
    return pl.pallas_call(
        _na_kernel,
        grid=(B, n_blocks // NA_BLOCKS_PER_STEP),
        in_specs=[
            pl.BlockSpec((None, step_q, NA_WIDTH), lambda b, i: (b, i, cq)),
            pl.BlockSpec((None, step_q, NA_WIDTH), lambda b, i: (b, i, cz)),
            pl.BlockSpec((None, S, NA_WIDTH), lambda b, i: (b, 0, ck)),
            pl.BlockSpec((None, S, NA_WIDTH), lambda b, i: (b, 0, cv)),
            pl.BlockSpec((None, C, NA_WIDTH), lambda b, i: (b, 0, ck)),
            pl.BlockSpec((None, C, NA_WIDTH), lambda b, i: (b, 0, cv)),
            pl.BlockSpec(bias_tabs.shape, lambda b, i: (0, 0, 0, 0), pipeline_mode=pl.Buffered(1)),
            pl.BlockSpec((1, NA_WIDTH), lambda b, i: (0, 0)),
        ],
        out_specs=pl.BlockSpec((None, step_q, NA_WIDTH), lambda b, i: (b, i, 0)),
        out_shape=jax.ShapeDtypeStruct((B, S, NA_WIDTH), _BF16),
        name="nbr_attn",
        compiler_params=_params("parallel", "parallel"),
    )(proj, proj, proj, proj, proj_ctx, proj_ctx, bias_tabs, na_g.reshape(1, NA_WIDTH))


def _ctx_attn_kernel(q_ref, z_ref, k_ref, v_ref, g_ref, o_ref):
    o = _softmax_pv(q_ref[...], [(k_ref[...], v_ref[...], None)])
    _branch_out(o, g_ref[...], 1.0, z_ref, o_ref)


def _ctx_attn(proj_ctx, na_g):
    B, C, _ = proj_ctx.shape
    return pl.pallas_call(
        _ctx_attn_kernel,
        grid=(B, NA_HEADS),
        in_specs=[
            pl.BlockSpec((None, C, HEAD_DIM), lambda b, h: (b, 0, COL_BQ + h)),
            pl.BlockSpec((None, C, HEAD_DIM), lambda b, h: (b, 0, COL_BZ + h)),
            pl.BlockSpec((None, C, HEAD_DIM), lambda b, h: (b, 0, COL_BK + h)),
            pl.BlockSpec((None, C, HEAD_DIM), lambda b, h: (b, 0, COL_BV + h)),
            pl.BlockSpec((None, 1, HEAD_DIM), lambda b, h: (h, 0, 0)),
        ],
        out_specs=pl.BlockSpec((None, C, HEAD_DIM), lambda b, h: (b, 0, h)),
        out_shape=jax.ShapeDtypeStruct((B, C, NA_WIDTH), _BF16),
        name="ctx_attn",
        compiler_params=_params("parallel", "parallel"),
    )(proj_ctx, proj_ctx, proj_ctx, proj_ctx, na_g.reshape(NA_HEADS, 1, HEAD_DIM))


def _gmlp_rows(rows, u_ref, v_ref, z_ref, lng_ref, lnb_ref, ws_ref, bs_ref, g_ref):
    n_chunks = (rows.stop - rows.start) // CHUNK
    groups = []
    for g in range(MLP_GROUPS):
        cols = slice(g * HEAD_DIM, (g + 1) * HEAD_DIM)
        w = ws_ref[g].astype(_BF16)
        bs = jnp.concatenate([bs_ref[g]] * n_chunks, axis=0)
        v = v_ref[rows, cols].astype(_F32)
        vc = v - jnp.mean(v, axis=-1, keepdims=True)
        vn = vc * lax.rsqrt(jnp.mean(vc * vc, axis=-1, keepdims=True) + EPS) * lng_ref[:, cols] + lnb_ref[:, cols]
        vn = vn.astype(_BF16)
        mixed = jnp.concatenate(
            [jnp.dot(w, vn[c * CHUNK:(c + 1) * CHUNK, :], preferred_element_type=_F32) for c in range(n_chunks)],
            axis=0) + bs
        om = u_ref[rows, cols].astype(_F32) * mixed
        z = z_ref[rows, cols].astype(_F32)
        groups.append((_rms(om, g_ref[:, cols]) * _silu(z)).astype(_BF16))
    return jnp.concatenate(groups, axis=1)


def _out_kernel(ya_ref, yn_ref, u_ref, v_ref, z_ref, lng_ref, lnb_ref, ws_ref, bs_ref, mg_ref,
                w_ref, x_ref, gate_ref, g_ref, o_ref):
    sub_tiles = [slice(r0, r0 + OUT_SUB_ROWS) for r0 in range(0, x_ref.shape[0], OUT_SUB_ROWS)]
    yms = [_gmlp_rows(rows, u_ref, v_ref, z_ref, lng_ref, lnb_ref, ws_ref, bs_ref, mg_ref) for rows in sub_tiles]
    for rows, ym in zip(sub_tiles, yms):
        y = jnp.dot(ya_ref[rows, :], w_ref[:DA_WIDTH, :], preferred_element_type=_F32)
        y = y + jnp.dot(yn_ref[rows, :], w_ref[DA_WIDTH:DA_WIDTH + NA_WIDTH, :], preferred_element_type=_F32)
        y = y + jnp.dot(ym, w_ref[DA_WIDTH + NA_WIDTH:, :], preferred_element_type=_F32)
        o_ref[rows, :] = x_ref[rows, :] + gate_ref[...] * _rms(y, g_ref[...])


def _out_proj(ya, yn, proj, mlp_params, w_bf16, layer, x2d, gate, post_g):
    T, D = x2d.shape
    nb = gate.shape[0]
    tm = min(OUT_TILE_ROWS, T // nb)
    tiles_per_mod = (T // nb) // tm
    ln_g, ln_b, ws, bs, out_g = mlp_params
    proj2d = proj.reshape(T, IN_WIDTH)
    vec = pl.BlockSpec((1, MLP_WIDTH), lambda i: (0, 0))
    return pl.pallas_call(
        _out_kernel,
        grid=(T // tm,),
        in_specs=[
            pl.BlockSpec((tm, DA_WIDTH), lambda i: (i, 0)),
            pl.BlockSpec((tm, NA_WIDTH), lambda i: (i, 0)),
            pl.BlockSpec((tm, MLP_WIDTH), lambda i: (i, COL_CU)),
            pl.BlockSpec((tm, MLP_WIDTH), lambda i: (i, COL_CV)),
            pl.BlockSpec((tm, MLP_WIDTH), lambda i: (i, COL_CZ)),
            vec,
            vec,
            pl.BlockSpec((MLP_GROUPS, CHUNK, CHUNK), lambda i: (0, 0, 0)),
            pl.BlockSpec((MLP_GROUPS, CHUNK, 1), lambda i: (0, 0, 0)),
            vec,
            pl.BlockSpec((None, MIX_WIDTH, D), lambda i: (layer, 0, 0), pipeline_mode=pl.Buffered(1)),
            pl.BlockSpec((tm, D), lambda i: (i, 0)),
            pl.BlockSpec((None, 1, D), lambda i: (i // tiles_per_mod, 0, 0)),
            pl.BlockSpec((1, D), lambda i: (0, 0)),
        ],
        out_specs=pl.BlockSpec((tm, D), lambda i: (i, 0)),
        out_shape=jax.ShapeDtypeStruct((T, D), _F32),
        name="out_proj",
        compiler_params=_params("parallel"),
    )(ya.reshape(T, DA_WIDTH), yn.reshape(T, NA_WIDTH), proj2d, proj2d, proj2d,
      ln_g.reshape(1, MLP_WIDTH), ln_b.reshape(1, MLP_WIDTH), ws, bs.reshape(MLP_GROUPS, CHUNK, 1),
      out_g.reshape(1, MLP_WIDTH), w_bf16, x2d, gate, post_g.reshape(1, D))


def _rope_tables(seq):
    n = DA_QK_DIM // 4
    freqs = ROPE_BASE ** (-np.arange(n, dtype=np.float64) / n)
    pos = np.arange(seq)
    row = (pos // GRID_W).astype(np.float64)[:, None] * freqs
    col = (pos % GRID_W).astype(np.float64)[:, None] * freqs
    zeros = np.zeros((seq, n))

    def one_map(fn_first, fn_second):
        return np.concatenate([fn_first(row), fn_second(row), fn_first(col), fn_second(col)], axis=-1)

    cos = one_map(np.cos, np.cos)
    sa = one_map(lambda a: -np.sin(a), lambda a: zeros)
    sb = one_map(lambda a: zeros, np.sin)
    return tuple(jnp.asarray(np.concatenate([t, t], axis=-1), _F32) for t in (cos, sa, sb))


def _identity_rope_tables(seq):
    return (jnp.ones((seq, HEAD_DIM), _F32), jnp.zeros((seq, HEAD_DIM), _F32),
            jnp.zeros((seq, HEAD_DIM), _F32))


def kernel(x, c, ctx, c_ctx, ada_w, ada_b, pre_g, post_g, w_in, w_out, lam_q1, lam_k1, lam_q2, lam_k2,
           da_g, na_rpb, na_g, mlp_ln_g, mlp_ln_b, mlp_ws, mlp_bs, mlp_g):
    B, S, D = x.shape
    C = ctx.shape[1]
    L = ada_w.shape[0]
    assert S % (NA_BLOCKS_PER_STEP * NA_Q) == 0 and S // GRID_W >= NA_KEY_ROWS and C % CHUNK == 0

    cc = jnp.concatenate([c, c_ctx[None], jnp.zeros((8 - B - 1, D), _F32)], axis=0)
    mod = _ada(cc, ada_w, ada_b)

    rope = _rope_tables(S)
    rope_id = _identity_rope_tables(C)
    w_in_b = w_in.astype(_BF16)
    w_out_b = w_out.astype(_BF16)
    x2d = x.reshape(B * S, D)
    xc2d = ctx.reshape(B * C, D)

    for l in range(L):
        lam_init = 0.8 - 0.6 * math.exp(-0.3 * l)
        ctx_out = l < L - 1
        shift, scale, gate = (mod[l, :B, i * D:(i + 1) * D].reshape(B, 1, D) for i in range(3))
        shift_c, scale_c, gate_c = (mod[l, B:B + 1, i * D:(i + 1) * D].reshape(1, 1, D) for i in range(3))
        lam_params = jnp.stack([lam_q1[l], lam_k1[l], lam_q2[l], lam_k2[l]])

        proj = _in_proj(x2d, pre_g[l], scale, shift, w_in_b, l, *rope, seq=S).reshape(B, S, IN_WIDTH)
        proj_c = _in_proj(xc2d, pre_g[l], scale_c, shift_c, w_in_b, l, *rope_id, seq=C,
                          kv_only=not ctx_out).reshape(B, C, IN_WIDTH)

        ya = _diff_attn(proj, [proj, proj_c], lam_params, da_g[l], lam_init=lam_init)
        yn = _nbr_attn(proj, proj_c, _na_bias_tables(na_rpb[l], S // GRID_W), na_g[l])
        mlp_params = (mlp_ln_g[l], mlp_ln_b[l], mlp_ws[l], mlp_bs[l], mlp_g[l])
        x2d_new = _out_proj(ya, yn, proj, mlp_params, w_out_b, l, x2d, gate, post_g[l])

        if ctx_out:
            yca = _diff_attn(proj_c, [proj_c], lam_params, da_g[l], lam_init=lam_init)
            ycn = _ctx_attn(proj_c, na_g[l])
            xc2d = _out_proj(yca, ycn, proj_c, mlp_params, w_out_b, l, xc2d, gate_c, post_g[l])
        x2d = x2d_new

    return x2d.reshape(B, S, D)
```

```python
import functools
import math

import numpy as np
import jax
import jax.numpy as jnp
from jax import lax
from jax.experimental import pallas as pl
from jax.experimental.pallas import tpu as pltpu

D_MODEL = 2048
DEPTH = 2
GRID_W = 64
HEAD_DIM = 128
DA_HEADS = 6
NA_HEADS = 6
MLP_GROUPS = 4
DA_WIDTH = DA_HEADS * HEAD_DIM
NA_WIDTH = NA_HEADS * HEAD_DIM
MLP_WIDTH = MLP_GROUPS * HEAD_DIM
DA_QK_DIM = HEAD_DIM // 2
NA_WIN_H = 8
NA_WIN_W = 16
CHUNK = 128
ROPE_BASE = 10000.0
EPS = 1e-6
IN_WIDTH = 4 * DA_WIDTH + 4 * NA_WIDTH + 3 * MLP_WIDTH
MIX_WIDTH = DA_WIDTH + NA_WIDTH + MLP_WIDTH

COL_AQ, COL_AK, COL_AV, COL_AZ = 0, 6, 12, 18
COL_BQ, COL_BK, COL_BV, COL_BZ = 24, 30, 36, 42
COL_CU, COL_CV, COL_CZ = 12, 13, 14

LOG2E = math.log2(math.e)
DA_SCALE = DA_QK_DIM ** -0.5 * LOG2E
NA_SCALE = HEAD_DIM ** -0.5 * LOG2E
DA_KEY_CHUNK = 512
DA_Q_TILE = 128
DA_TILES_PER_STEP = 32


def _in_segments():
    kinds = ["rope_scale", "rope", "plain", "plain", "na_scale", "plain", "plain", "plain", "gelu", "gelu", "plain"]
    widths = [DA_WIDTH] * 4 + [NA_WIDTH] * 4 + [MLP_WIDTH] * 3
    segs, c0 = [], 0
    for width, kind in zip(widths, kinds):
        segs.append((c0, width, kind))
        c0 += width
    return tuple(segs)


IN_SEGMENTS = _in_segments()
IN_KV_SEGMENTS = (1, 2, 5, 6)
IN_TILE_ROWS = 256
OUT_TILE_ROWS = 512
OUT_SUB_ROWS = 256

NA_BLOCK_ROWS = 4
NA_KEY_ROWS = NA_BLOCK_ROWS + NA_WIN_H
NA_BLOCKS_PER_STEP = 2
NA_Q = NA_BLOCK_ROWS * GRID_W
NA_K = NA_KEY_ROWS * GRID_W
MASK_VALUE = -1e30

VMEM_LIMIT_BYTES = 56 * 1024 * 1024

_BF16 = jnp.bfloat16
_F32 = jnp.float32


def _silu(x):
    return x * (1.0 / (1.0 + jnp.exp(-x)))


def _gelu_tanh(x):
    c = math.sqrt(2.0 / math.pi)
    return 0.5 * x * (1.0 + jnp.tanh(c * (x + 0.044715 * (x * x * x))))


def _rms(x, g):
    return x * lax.rsqrt(jnp.mean(x * x, axis=-1, keepdims=True) + EPS) * g


def _dot_nt(a, b):
    return lax.dot_general(a, b, (((1,), (1,)), ((), ())), preferred_element_type=_F32)


def _params(*sem):
    return pltpu.CompilerParams(dimension_semantics=sem, vmem_limit_bytes=VMEM_LIMIT_BYTES)


def _ada_kernel(c_ref, w_ref, b_ref, o_ref):
    sc = _silu(c_ref[...]).astype(_BF16)
    o_ref[...] = jnp.dot(sc, w_ref[...].astype(_BF16), preferred_element_type=_F32) + b_ref[...]


def _ada(cc, ada_w, ada_b):
    L, D, N = ada_w.shape
    R = cc.shape[0]
    tn = 768
    return pl.pallas_call(
        _ada_kernel,
        grid=(L, N // tn),
        in_specs=[
            pl.BlockSpec((R, D), lambda l, j: (0, 0)),
            pl.BlockSpec((None, D, tn), lambda l, j: (l, 0, j)),
            pl.BlockSpec((None, 1, tn), lambda l, j: (l, 0, j)),
        ],
        out_specs=pl.BlockSpec((None, R, tn), lambda l, j: (l, 0, j)),
        out_shape=jax.ShapeDtypeStruct((L, R, N), _F32),
        name="ada_mod",
        compiler_params=_params("parallel", "parallel"),
    )(cc, ada_w, ada_b.reshape(L, 1, N))


def _in_kernel(x_ref, g_ref, scale_ref, shift_ref, w_ref, cos_ref, sa_ref, sb_ref, o_ref, *, kv_only):
    x = x_ref[...]
    h = (_rms(x, g_ref[...]) * (1.0 + scale_ref[...]) + shift_ref[...]).astype(_BF16)
    cos, sa, sb = cos_ref[...], sa_ref[...], sb_ref[...]
    for idx, (c0, width, epilogue) in enumerate(IN_SEGMENTS):
        if kv_only and idx not in IN_KV_SEGMENTS:
            o_ref[:, c0:c0 + width] = jnp.zeros((o_ref.shape[0], width), _BF16)
            continue
        acc = jnp.dot(h, w_ref[:, c0:c0 + width], preferred_element_type=_F32)
        if epilogue in ("rope", "rope_scale"):
            for hh in range(width // HEAD_DIM):
                xs = acc[:, hh * HEAD_DIM:(hh + 1) * HEAD_DIM]
                r = xs * cos + pltpu.roll(xs, HEAD_DIM - 16, 1) * sa + pltpu.roll(xs, 16, 1) * sb
                if epilogue == "rope_scale":
                    r = r * DA_SCALE
                o_ref[:, c0 + hh * HEAD_DIM:c0 + (hh + 1) * HEAD_DIM] = r.astype(_BF16)
        elif epilogue == "na_scale":
            o_ref[:, c0:c0 + width] = (acc * NA_SCALE).astype(_BF16)
        elif epilogue == "gelu":
            o_ref[:, c0:c0 + width] = _gelu_tanh(acc).astype(_BF16)
        else:
            o_ref[:, c0:c0 + width] = acc.astype(_BF16)


def _in_proj(x2d, pre_g, scale, shift, w_bf16, layer, cos, sa, sb, *, seq, kv_only=False):
    T, D = x2d.shape
    nb = scale.shape[0]
    tm = IN_TILE_ROWS
    tiles_per_mod = (T // nb) // tm
    tiles_per_seq = seq // tm
    return pl.pallas_call(
        functools.partial(_in_kernel, kv_only=kv_only),
        grid=(T // tm,),
        in_specs=[
            pl.BlockSpec((tm, D), lambda i: (i, 0)),
            pl.BlockSpec((1, D), lambda i: (0, 0)),
            pl.BlockSpec((None, 1, D), lambda i: (i // tiles_per_mod, 0, 0)),
            pl.BlockSpec((None, 1, D), lambda i: (i // tiles_per_mod, 0, 0)),
            pl.BlockSpec((None, D, IN_WIDTH), lambda i: (layer, 0, 0), pipeline_mode=pl.Buffered(1)),
            pl.BlockSpec((tm, HEAD_DIM), lambda i: (i % tiles_per_seq, 0)),
            pl.BlockSpec((tm, HEAD_DIM), lambda i: (i % tiles_per_seq, 0)),
            pl.BlockSpec((tm, HEAD_DIM), lambda i: (i % tiles_per_seq, 0)),
        ],
        out_specs=pl.BlockSpec((tm, IN_WIDTH), lambda i: (i, 0)),
        out_shape=jax.ShapeDtypeStruct((T, IN_WIDTH), _BF16),
        name="in_proj",
        compiler_params=_params("parallel"),
    )(x2d, pre_g.reshape(1, D), scale, shift, w_bf16, cos, sa, sb)


def _softmax_pv(qq, sources):
    scores = []
    for k, _, bias in sources:
        s = _dot_nt(qq, k)
        if bias is not None:
            s = s + bias
        scores.append(s)
    m = jnp.max(scores[0], axis=-1, keepdims=True)
    for s in scores[1:]:
        m = jnp.maximum(m, jnp.max(s, axis=-1, keepdims=True))
    acc = None
    for s, (_, v, _) in zip(scores, sources):
        p = jnp.exp2(s - m).astype(_BF16)
        v_aug = jnp.concatenate([v, jnp.ones(v.shape, _BF16)], axis=1)
        pv = jnp.dot(p, v_aug, preferred_element_type=_F32)
        acc = pv if acc is None else acc + pv
    return acc[:, :HEAD_DIM] / acc[:, HEAD_DIM:]


def _branch_out(o, g, factor, z_ref, o_ref):
    z = z_ref[...].astype(_F32)
    o_ref[...] = (_rms(o, g) * factor * _silu(z)).astype(_BF16)


def _da_kernel(*refs, n_src, lam_init):
    q_ref, z_ref = refs[0], refs[1]
    kv_refs = refs[2:2 + 2 * n_src]
    lam_ref, g_ref, o_ref, s_ref = refs[2 + 2 * n_src:]
    tq = DA_Q_TILE
    n_tiles = q_ref.shape[0] // tq

    chunks = []
    col = 0
    for i in range(n_src):
        k_ref, v_ref = kv_refs[2 * i], kv_refs[2 * i + 1]
        size = k_ref.shape[0]
        ck = min(size, DA_KEY_CHUNK)
        for c in range(size // ck):
            chunks.append((k_ref, v_ref, c * ck, ck, col))
            col += ck

    lp = lam_ref[...]
    lam = (jnp.exp(jnp.sum(lp[0:1] * lp[1:2], axis=-1, keepdims=True))
           - jnp.exp(jnp.sum(lp[2:3] * lp[3:4], axis=-1, keepdims=True)) + lam_init)

    def scores_and_max(t):
        q = q_ref[t * tq:(t + 1) * tq, :]
        lane = lax.broadcasted_iota(jnp.int32, q.shape, 1)
        zero = jnp.zeros_like(q)
        qq = jnp.concatenate([jnp.where(lane < DA_QK_DIM, q, zero),
                              jnp.where(lane >= DA_QK_DIM, q, zero)], axis=0)
        mx = None
        for k_ref, _, r0, ck, c0 in chunks:
            s = _dot_nt(qq, k_ref[r0:r0 + ck, :])
            s_ref[t % 2, :, c0:c0 + ck] = s
            for jj in range(ck // HEAD_DIM):
                part = s[:, jj * HEAD_DIM:(jj + 1) * HEAD_DIM]
                mx = part if mx is None else jnp.maximum(mx, part)
        return jnp.max(mx, axis=-1, keepdims=True)

    def softmax_pv_out(t, m):
        acc = None
        for _, v_ref, r0, ck, c0 in chunks:
            p = jnp.exp2(s_ref[t % 2, :, c0:c0 + ck] - m).astype(_BF16)
            v_aug = jnp.concatenate([v_ref[r0:r0 + ck, :], jnp.ones((ck, HEAD_DIM), _BF16)], axis=1)
            pv = jnp.dot(p, v_aug, preferred_element_type=_F32)
            acc = pv if acc is None else acc + pv
        o = acc[:, :HEAD_DIM] / acc[:, HEAD_DIM:]
        oa = o[:tq] - lam * o[tq:]
        rows = slice(t * tq, (t + 1) * tq)
        z = z_ref[rows, :].astype(_F32)
        o_ref[rows, :] = (_rms(oa, g_ref[...]) * (1.0 - lam_init) * _silu(z)).astype(_BF16)

    row_max = scores_and_max(0)
    for t in range(1, n_tiles):
        next_max = scores_and_max(t)
        softmax_pv_out(t - 1, row_max)
        row_max = next_max
    softmax_pv_out(n_tiles - 1, row_max)


def _diff_attn(q_proj, kv_projs, lam_params, da_g, *, lam_init):
    B, Sq, _ = q_proj.shape
    tq = min(DA_TILES_PER_STEP * DA_Q_TILE, Sq)
    in_specs = [
        pl.BlockSpec((None, tq, HEAD_DIM), lambda b, h, i: (b, i, COL_AQ + h)),
        pl.BlockSpec((None, tq, HEAD_DIM), lambda b, h, i: (b, i, COL_AZ + h)),
    ]
    args = [q_proj, q_proj]
    for kp in kv_projs:
        sk = kp.shape[1]
        in_specs.append(pl.BlockSpec((None, sk, HEAD_DIM), lambda b, h, i: (b, 0, COL_AK + h)))
        in_specs.append(pl.BlockSpec((None, sk, HEAD_DIM), lambda b, h, i: (b, 0, COL_AV + h)))
        args += [kp, kp]
    in_specs += [
        pl.BlockSpec((4, DA_QK_DIM), lambda b, h, i: (0, 0)),
        pl.BlockSpec((1, HEAD_DIM), lambda b, h, i: (0, 0)),
    ]
    args += [lam_params, da_g.reshape(1, HEAD_DIM)]
    return pl.pallas_call(
        functools.partial(_da_kernel, n_src=len(kv_projs), lam_init=lam_init),
        grid=(B, DA_HEADS, Sq // tq),
        in_specs=in_specs,
        out_specs=pl.BlockSpec((None, tq, HEAD_DIM), lambda b, h, i: (b, i, h)),
        out_shape=jax.ShapeDtypeStruct((B, Sq, DA_WIDTH), _BF16),
        scratch_shapes=[pltpu.VMEM((2, 2 * DA_Q_TILE, sum(kp.shape[1] for kp in kv_projs)), _F32)],
        name="diff_attn",
        compiler_params=_params("parallel", "parallel", "parallel"),
    )(*args)


def _na_kernel(q_ref, z_ref, k_ref, v_ref, kc_ref, vc_ref, bias_ref, g_ref, o_ref):
    step = pl.program_id(1)
    rows = k_ref.shape[0] // GRID_W
    n_blocks = rows // NA_BLOCK_ROWS
    n_ctx = kc_ref.shape[0]
    ones = jnp.ones((NA_K, HEAD_DIM), _BF16)

    def key_start(j):
        rb = step * NA_BLOCKS_PER_STEP + j
        kr0 = jnp.clip(rb * NA_BLOCK_ROWS - NA_WIN_H // 2, 0, rows - NA_KEY_ROWS)
        return pl.multiple_of(kr0 * GRID_W, GRID_W)

    def bias_table(j):
        rb = step * NA_BLOCKS_PER_STEP + j
        return jnp.where(rb == 0, 0, jnp.where(rb == n_blocks - 1, 2, 1))

    def scores_and_max(j, h):
        cols = slice(h * HEAD_DIM, (h + 1) * HEAD_DIM)
        q = q_ref[j * NA_Q:(j + 1) * NA_Q, cols]
        s_loc = _dot_nt(q, k_ref[pl.ds(key_start(j), NA_K), cols]) + bias_ref[bias_table(j), h]
        s_ctx = _dot_nt(q, kc_ref[:, cols])
        m = jnp.maximum(jnp.max(s_loc, axis=-1, keepdims=True), jnp.max(s_ctx, axis=-1, keepdims=True))
        return s_loc, s_ctx, m

    def softmax_pv_out(j, h, s_loc, s_ctx, m):
        cols = slice(h * HEAD_DIM, (h + 1) * HEAD_DIM)
        v_loc = jnp.concatenate([v_ref[pl.ds(key_start(j), NA_K), cols], ones], axis=1)
        v_ctx = jnp.concatenate([vc_ref[:, cols], ones[:n_ctx]], axis=1)
        acc = jnp.dot(jnp.exp2(s_loc - m).astype(_BF16), v_loc, preferred_element_type=_F32)
        acc = acc + jnp.dot(jnp.exp2(s_ctx - m).astype(_BF16), v_ctx, preferred_element_type=_F32)
        o = acc[:, :HEAD_DIM] / acc[:, HEAD_DIM:]
        qrows = slice(j * NA_Q, (j + 1) * NA_Q)
        z = z_ref[qrows, cols].astype(_F32)
        o_ref[qrows, cols] = (_rms(o, g_ref[:, cols]) * _silu(z)).astype(_BF16)

    units = [(j, h) for j in range(NA_BLOCKS_PER_STEP) for h in range(NA_HEADS)]
    pending = scores_and_max(*units[0])
    for u in range(1, len(units)):
        nxt = scores_and_max(*units[u])
        softmax_pv_out(*units[u - 1], *pending)
        pending = nxt
    softmax_pv_out(*units[-1], *pending)


def _na_bias_kernel(r_ref, o_ref, *, rows):
    shape = (GRID_W, 2 * GRID_W)
    c = lax.broadcasted_iota(jnp.int32, shape, 0)
    lane = lax.broadcasted_iota(jnp.int32, shape, 1)
    kc = lane & (GRID_W - 1)
    cs = jnp.clip(c - NA_WIN_W // 2, 0, GRID_W - NA_WIN_W)
    col_valid = (kc >= cs) & (kc < cs + NA_WIN_W)
    first_key_row = lane < GRID_W
    toeplitz = []
    for t in range(2 * NA_WIN_H):
        row = jnp.broadcast_to(r_ref[t:t + 1, :], shape)
        x = pltpu.roll(row, 2 * GRID_W - (NA_WIN_W - 1), 1, stride=1, stride_axis=0)
        toeplitz.append(jnp.where(col_valid, x * LOG2E, MASK_VALUE))
    masked = jnp.full(shape, MASK_VALUE, _F32)
    for cfg, (r0, kr0) in enumerate(_na_block_configs(rows)):
        for ri in range(NA_BLOCK_ROWS):
            r = r0 + ri
            rs = min(max(r - NA_WIN_H // 2, 0), rows - NA_WIN_H)
            for j in range(NA_KEY_ROWS // 2):
                kra = kr0 + 2 * j
                valid_a = rs <= kra < rs + NA_WIN_H
                valid_b = rs <= kra + 1 < rs + NA_WIN_H
                t = kra - r + NA_WIN_H
                if valid_a and valid_b:
                    blk = toeplitz[t]
                elif valid_a:
                    blk = jnp.where(first_key_row, toeplitz[t], MASK_VALUE)
                elif valid_b:
                    blk = jnp.where(first_key_row, MASK_VALUE, toeplitz[t])
                else:
                    blk = masked
                o_ref[cfg, ri * GRID_W:(ri + 1) * GRID_W, j * 2 * GRID_W:(j + 1) * 2 * GRID_W] = blk


def _na_block_configs(rows):
    return ((0, 0), (2 * NA_BLOCK_ROWS, 2 * NA_BLOCK_ROWS - NA_WIN_H // 2),
            (rows - NA_BLOCK_ROWS, rows - NA_KEY_ROWS))


def _na_bias_tables(rpb, rows):
    H, nr, ncol = rpb.shape
    lo = jnp.pad(rpb, ((0, 0), (1, 0), (0, 2 * GRID_W - ncol)))
    hi = jnp.pad(rpb, ((0, 0), (0, 1), (GRID_W, GRID_W - ncol)))
    return pl.pallas_call(
        functools.partial(_na_bias_kernel, rows=rows),
        grid=(H,),
        in_specs=[pl.BlockSpec((None, nr + 1, 2 * GRID_W), lambda h: (h, 0, 0))],
        out_specs=pl.BlockSpec((3, None, NA_Q, NA_K), lambda h: (0, h, 0, 0)),
        out_shape=jax.ShapeDtypeStruct((3, H, NA_Q, NA_K), _F32),
        name="na_bias",
        compiler_params=_params("parallel"),
    )(lo + hi)


def _nbr_attn(proj, proj_ctx, bias_tabs, na_g):
    B, S, _ = proj.shape
    C = proj_ctx.shape[1]
    n_blocks = S // NA_Q
    step_q = NA_BLOCKS_PER_STEP * NA_Q

    cq, ck, cv, cz = (c * HEAD_DIM // NA_WIDTH for c in (COL_BQ, COL_BK, COL_BV, COL_BZ))
    return pl.pallas_call(
        _na_kernel,
        grid=(B, n_blocks // NA_BLOCKS_PER_STEP),
        in_specs=[
            pl.BlockSpec((None, step_q, NA_WIDTH), lambda b, i: (b, i, cq)),
            pl.BlockSpec((None, step_q, NA_WIDTH), lambda b, i: (b, i, cz)),
            pl.BlockSpec((None, S, NA_WIDTH), lambda b, i: (b, 0, ck)),
            pl.BlockSpec((None, S, NA_WIDTH), lambda b, i: (b, 0, cv)),
            pl.BlockSpec((None, C, NA_WIDTH), lambda b, i: (b, 0, ck)),
            pl.BlockSpec((None, C, NA_WIDTH), lambda b, i: (b, 0, cv)),
            pl.BlockSpec(bias_tabs.shape, lambda b, i: (0, 0, 0, 0), pipeline_mode=pl.Buffered(1)),
            pl.BlockSpec((1, NA_WIDTH), lambda b, i: (0, 0)),
        ],
        out_specs=pl.BlockSpec((None, step_q, NA_WIDTH), lambda b, i: (b, i, 0)),
        out_shape=jax.ShapeDtypeStruct((B, S, NA_WIDTH), _BF16),
        name="nbr_attn",
        compiler_params=_params("parallel", "parallel"),
    )(proj, proj, proj, proj, proj_ctx, proj_ctx, bias_tabs, na_g.reshape(1, NA_WIDTH))


def _ctx_attn_kernel(q_ref, z_ref, k_ref, v_ref, g_ref, o_ref):
    o = _softmax_pv(q_ref[...], [(k_ref[...], v_ref[...], None)])
    _branch_out(o, g_ref[...], 1.0, z_ref, o_ref)


def _ctx_attn(proj_ctx, na_g):
    B, C, _ = proj_ctx.shape
    return pl.pallas_call(
        _ctx_attn_kernel,
        grid=(B, NA_HEADS),
        in_specs=[
            pl.BlockSpec((None, C, HEAD_DIM), lambda b, h: (b, 0, COL_BQ + h)),
            pl.BlockSpec((None, C, HEAD_DIM), lambda b, h: (b, 0, COL_BZ + h)),
            pl.BlockSpec((None, C, HEAD_DIM), lambda b, h: (b, 0, COL_BK + h)),
            pl.BlockSpec((None, C, HEAD_DIM), lambda b, h: (b, 0, COL_BV + h)),
            pl.BlockSpec((None, 1, HEAD_DIM), lambda b, h: (h, 0, 0)),
        ],
        out_specs=pl.BlockSpec((None, C, HEAD_DIM), lambda b, h: (b, 0, h)),
        out_shape=jax.ShapeDtypeStruct((B, C, NA_WIDTH), _BF16),
        name="ctx_attn",
        compiler_params=_params("parallel", "parallel"),
    )(proj_ctx, proj_ctx, proj_ctx, proj_ctx, na_g.reshape(NA_HEADS, 1, HEAD_DIM))


def _gmlp_rows(rows, u_ref, v_ref, z_ref, lng_ref, lnb_ref, ws_ref, bs_ref, g_ref):
    n_chunks = (rows.stop - rows.start) // CHUNK
    groups = []
    for g in range(MLP_GROUPS):
        cols = slice(g * HEAD_DIM, (g + 1) * HEAD_DIM)
        w = ws_ref[g].astype(_BF16)
        bs = jnp.concatenate([bs_ref[g]] * n_chunks, axis=0)
        v = v_ref[rows, cols].astype(_F32)
        vc = v - jnp.mean(v, axis=-1, keepdims=True)
        vn = vc * lax.rsqrt(jnp.mean(vc * vc, axis=-1, keepdims=True) + EPS) * lng_ref[:, cols] + lnb_ref[:, cols]
        vn = vn.astype(_BF16)
        mixed = jnp.concatenate(
            [jnp.dot(w, vn[c * CHUNK:(c + 1) * CHUNK, :], preferred_element_type=_F32) for c in range(n_chunks)],
            axis=0) + bs
        om = u_ref[rows, cols].astype(_F32) * mixed
        z = z_ref[rows, cols].astype(_F32)
        groups.append((_rms(om, g_ref[:, cols]) * _silu(z)).astype(_BF16))
    return jnp.concatenate(groups, axis=1)


def _out_kernel(ya_ref, yn_ref, u_ref, v_ref, z_ref, lng_ref, lnb_ref, ws_ref, bs_ref, mg_ref,
                w_ref, x_ref, gate_ref, g_ref, o_ref):
    sub_tiles = [slice(r0, r0 + OUT_SUB_ROWS) for r0 in range(0, x_ref.shape[0], OUT_SUB_ROWS)]
    yms = [_gmlp_rows(rows, u_ref, v_ref, z_ref, lng_ref, lnb_ref, ws_ref, bs_ref, mg_ref) for rows in sub_tiles]
    for rows, ym in zip(sub_tiles, yms):
        y = jnp.dot(ya_ref[rows, :], w_ref[:DA_WIDTH, :], preferred_element_type=_F32)
        y = y + jnp.dot(yn_ref[rows, :], w_ref[DA_WIDTH:DA_WIDTH + NA_WIDTH, :], preferred_element_type=_F32)
        y = y + jnp.dot(ym, w_ref[DA_WIDTH + NA_WIDTH:, :], preferred_element_type=_F32)
        o_ref[rows, :] = x_ref[rows, :] + gate_ref[...] * _rms(y, g_ref[...])


def _out_proj(ya, yn, proj, mlp_params, w_bf16, layer, x2d, gate, post_g):
    T, D = x2d.shape
    nb = gate.shape[0]
    tm = min(OUT_TILE_ROWS, T // nb)
    tiles_per_mod = (T // nb) // tm
    ln_g, ln_b, ws, bs, out_g = mlp_params
    proj2d = proj.reshape(T, IN_WIDTH)
    vec = pl.BlockSpec((1, MLP_WIDTH), lambda i: (0, 0))
    return pl.pallas_call(
        _out_kernel,
        grid=(T // tm,),
        in_specs=[
            pl.BlockSpec((tm, DA_WIDTH), lambda i: (i, 0)),
            pl.BlockSpec((tm, NA_WIDTH), lambda i: (i, 0)),
            pl.BlockSpec((tm, MLP_WIDTH), lambda i: (i, COL_CU)),
            pl.BlockSpec((tm, MLP_WIDTH), lambda i: (i, COL_CV)),
            pl.BlockSpec((tm, MLP_WIDTH), lambda i: (i, COL_CZ)),
            vec,
            vec,
            pl.BlockSpec((MLP_GROUPS, CHUNK, CHUNK), lambda i: (0, 0, 0)),
            pl.BlockSpec((MLP_GROUPS, CHUNK, 1), lambda i: (0, 0, 0)),
            vec,
            pl.BlockSpec((None, MIX_WIDTH, D), lambda i: (layer, 0, 0), pipeline_mode=pl.Buffered(1)),
            pl.BlockSpec((tm, D), lambda i: (i, 0)),
            pl.BlockSpec((None, 1, D), lambda i: (i // tiles_per_mod, 0, 0)),
            pl.BlockSpec((1, D), lambda i: (0, 0)),
        ],
        out_specs=pl.BlockSpec((tm, D), lambda i: (i, 0)),
        out_shape=jax.ShapeDtypeStruct((T, D), _F32),
        name="out_proj",
        compiler_params=_params("parallel"),
    )(ya.reshape(T, DA_WIDTH), yn.reshape(T, NA_WIDTH), proj2d, proj2d, proj2d,
      ln_g.reshape(1, MLP_WIDTH), ln_b.reshape(1, MLP_WIDTH), ws, bs.reshape(MLP_GROUPS, CHUNK, 1),
      out_g.reshape(1, MLP_WIDTH), w_bf16, x2d, gate, post_g.reshape(1, D))


def _rope_tables(seq):
    n = DA_QK_DIM // 4
    freqs = ROPE_BASE ** (-np.arange(n, dtype=np.float64) / n)
    pos = np.arange(seq)
    row = (pos // GRID_W).astype(np.float64)[:, None] * freqs
    col = (pos % GRID_W).astype(np.float64)[:, None] * freqs
    zeros = np.zeros((seq, n))

    def one_map(fn_first, fn_second):
        return np.concatenate([fn_first(row), fn_second(row), fn_first(col), fn_second(col)], axis=-1)

    cos = one_map(np.cos, np.cos)
    sa = one_map(lambda a: -np.sin(a), lambda a: zeros)
    sb = one_map(lambda a: zeros, np.sin)
    return tuple(jnp.asarray(np.concatenate([t, t], axis=-1), _F32) for t in (cos, sa, sb))


def _identity_rope_tables(seq):
    return (jnp.ones((seq, HEAD_DIM), _F32), jnp.zeros((seq, HEAD_DIM), _F32),
            jnp.zeros((seq, HEAD_DIM), _F32))


def kernel(x, c, ctx, c_ctx, ada_w, ada_b, pre_g, post_g, w_in, w_out, lam_q1, lam_k1, lam_q2, lam_k2,
           da_g, na_rpb, na_g, mlp_ln_g, mlp_ln_b, mlp_ws, mlp_bs, mlp_g):
    B, S, D = x.shape
    C = ctx.shape[1]
    L = ada_w.shape[0]
    assert S % (NA_BLOCKS_PER_STEP * NA_Q) == 0 and S // GRID_W >= NA_KEY_ROWS and C % CHUNK == 0

    cc = jnp.concatenate([c, c_ctx[None], jnp.zeros((8 - B - 1, D), _F32)], axis=0)
    mod = _ada(cc, ada_w, ada_b)

    rope = _rope_tables(S)
    rope_id = _identity_rope_tables(C)
    w_in_b = w_in.astype(_BF16)
    w_out_b = w_out.astype(_BF16)
    x2d = x.reshape(B * S, D)
    xc2d = ctx.reshape(B * C, D)

    for l in range(L):
        lam_init = 0.8 - 0.6 * math.exp(-0.3 * l)
        ctx_out = l < L - 1
        shift, scale, gate = (mod[l, :B, i * D:(i + 1) * D].reshape(B, 1, D) for i in range(3))
        shift_c, scale_c, gate_c = (mod[l, B:B + 1, i * D:(i + 1) * D].reshape(1, 1, D) for i in range(3))
        lam_params = jnp.stack([lam_q1[l], lam_k1[l], lam_q2[l], lam_k2[l]])

        proj = _in_proj(x2d, pre_g[l], scale, shift, w_in_b, l, *rope, seq=S).reshape(B, S, IN_WIDTH)
        proj_c = _in_proj(xc2d, pre_g[l], scale_c, shift_c, w_in_b, l, *rope_id, seq=C,
                          kv_only=not ctx_out).reshape(B, C, IN_WIDTH)

        ya = _diff_attn(proj, [proj, proj_c], lam_params, da_g[l], lam_init=lam_init)
        yn = _nbr_attn(proj, proj_c, _na_bias_tables(na_rpb[l], S // GRID_W), na_g[l])
        mlp_params = (mlp_ln_g[l], mlp_ln_b[l], mlp_ws[l], mlp_bs[l], mlp_g[l])
        x2d_new = _out_proj(ya, yn, proj, mlp_params, w_out_b, l, x2d, gate, post_g[l])

        if ctx_out:
            yca = _diff_attn(proj_c, [proj_c], lam_params, da_g[l], lam_init=lam_init)
            ycn = _ctx_attn(proj_c, na_g[l])
            xc2d = _out_proj(yca, ycn, proj_c, mlp_params, w_out_b, l, xc2d, gate_c, post_g[l])
        x2d = x2d_new

    return x2d.reshape(B, S, D)
```

```python
import functools
import math

import numpy as np
import jax
import jax.numpy as jnp
from jax import lax
from jax.experimental import pallas as pl
from jax.experimental.pallas import tpu as pltpu

D_MODEL = 2048
DEPTH = 2
GRID_W = 64
HEAD_DIM = 128
DA_HEADS = 6
NA_HEADS = 6
MLP_GROUPS = 4
DA_WIDTH = DA_HEADS * HEAD_DIM
NA_WIDTH = NA_HEADS * HEAD_DIM
MLP_WIDTH = MLP_GROUPS * HEAD_DIM
DA_QK_DIM = HEAD_DIM // 2
NA_WIN_H = 8
NA_WIN_W = 16
CHUNK = 128
ROPE_BASE = 10000.0
EPS = 1e-6
IN_WIDTH = 4 * DA_WIDTH + 4 * NA_WIDTH + 3 * MLP_WIDTH
MIX_WIDTH = DA_WIDTH + NA_WIDTH + MLP_WIDTH

COL_AQ, COL_AK, COL_AV, COL_AZ = 0, 6, 12, 18
COL_BQ, COL_BK, COL_BV, COL_BZ = 24, 30, 36, 42
COL_CU, COL_CV, COL_CZ = 12, 13, 14

ROPE_PAIR_DIST = DA_QK_DIM // 4
LOG2E = math.log2(math.e)
DA_SCALE = DA_QK_DIM ** -0.5 * LOG2E
NA_SCALE = HEAD_DIM ** -0.5 * LOG2E
DA_KEY_CHUNK = 512
DA_Q_TILE = 128
DA_TILES_PER_STEP = 16


def _in_segments():
    kinds = ["rope_scale", "rope", "plain", "plain", "na_scale", "plain", "plain", "plain", "gelu", "gelu", "plain"]
    widths = [DA_WIDTH] * 4 + [NA_WIDTH] * 4 + [MLP_WIDTH] * 3
    segs, c0 = [], 0
    for width, kind in zip(widths, kinds):
        segs.append((c0, width, kind))
        c0 += width
    return tuple(segs)


ADA_TILE_COLS = 768
ADA_ROWS = 8
IN_SEGMENTS = _in_segments()
IN_KV_SEGMENTS = (1, 2, 5, 6)
IN_TILE_ROWS = 256
OUT_TILE_ROWS = 512
OUT_SUB_ROWS = 256

NA_BLOCK_ROWS = 4
NA_KEY_ROWS = NA_BLOCK_ROWS + NA_WIN_H
NA_BLOCKS_PER_STEP = 2
NA_Q = NA_BLOCK_ROWS * GRID_W
NA_K = NA_KEY_ROWS * GRID_W
MASK_VALUE = -1e30

VMEM_LIMIT_BYTES = 56 * 1024 * 1024

_BF16 = jnp.bfloat16
_F32 = jnp.float32


def _silu(x):
    return x * (1.0 / (1.0 + jnp.exp(-x)))


def _gelu_tanh(x):
    c = math.sqrt(2.0 / math.pi)
    return 0.5 * x * (1.0 + jnp.tanh(c * (x + 0.044715 * (x * x * x))))


def _rms(x, g):
    return x * lax.rsqrt(jnp.mean(x * x, axis=-1, keepdims=True) + EPS) * g


def _dot_nt(a, b):
    return lax.dot_general(a, b, (((1,), (1,)), ((), ())), preferred_element_type=_F32)


def _params(*sem):
    return pltpu.CompilerParams(dimension_semantics=sem, vmem_limit_bytes=VMEM_LIMIT_BYTES)


def _ada_kernel(c_ref, w_ref, b_ref, o_ref):
    sc = _silu(c_ref[...]).astype(_BF16)
    o_ref[...] = jnp.dot(sc, w_ref[...].astype(_BF16), preferred_element_type=_F32) + b_ref[...]


def _ada(cc, ada_w, ada_b):
    L, D, N = ada_w.shape
    R = cc.shape[0]
    tn = ADA_TILE_COLS
    return pl.pallas_call(
        _ada_kernel,
        grid=(L, N // tn),
        in_specs=[
            pl.BlockSpec((R, D), lambda l, j: (0, 0)),
            pl.BlockSpec((None, D, tn), lambda l, j: (l, 0, j)),
            pl.BlockSpec((None, 1, tn), lambda l, j: (l, 0, j)),
        ],
        out_specs=pl.BlockSpec((None, R, tn), lambda l, j: (l, 0, j)),
        out_shape=jax.ShapeDtypeStruct((L, R, N), _F32),
        name="ada_mod",
        compiler_params=_params("parallel", "parallel"),
    )(cc, ada_w, ada_b.reshape(L, 1, N))


def _in_kernel(x_ref, g_ref, scale_ref, shift_ref, w_ref, cos_ref, sa_ref, sb_ref, o_ref, *, kv_only):
    x = x_ref[...]
    h = (_rms(x, g_ref[...]) * (1.0 + scale_ref[...]) + shift_ref[...]).astype(_BF16)
    cos, sa, sb = cos_ref[...], sa_ref[...], sb_ref[...]
    for idx, (c0, width, epilogue) in enumerate(IN_SEGMENTS):
        if kv_only and idx not in IN_KV_SEGMENTS:
            o_ref[:, c0:c0 + width] = jnp.zeros((o_ref.shape[0], width), _BF16)
            continue
        acc = jnp.dot(h, w_ref[:, c0:c0 + width], preferred_element_type=_F32)
        if epilogue in ("rope", "rope_scale"):
            for hh in range(width // HEAD_DIM):
                xs = acc[:, hh * HEAD_DIM:(hh + 1) * HEAD_DIM]
                r = (xs * cos + pltpu.roll(xs, HEAD_DIM - ROPE_PAIR_DIST, 1) * sa
                     + pltpu.roll(xs, ROPE_PAIR_DIST, 1) * sb)
                if epilogue == "rope_scale":
                    r = r * DA_SCALE
                o_ref[:, c0 + hh * HEAD_DIM:c0 + (hh + 1) * HEAD_DIM] = r.astype(_BF16)
        elif epilogue == "na_scale":
            o_ref[:, c0:c0 + width] = (acc * NA_SCALE).astype(_BF16)
        elif epilogue == "gelu":
            o_ref[:, c0:c0 + width] = _gelu_tanh(acc).astype(_BF16)
        else:
            o_ref[:, c0:c0 + width] = acc.astype(_BF16)


def _in_proj(x2d, pre_g, scale, shift, w_bf16, layer, cos, sa, sb, *, seq, kv_only=False):
    T, D = x2d.shape
    nb = scale.shape[0]
    tm = IN_TILE_ROWS
    tiles_per_mod = (T // nb) // tm
    tiles_per_seq = seq // tm
    return pl.pallas_call(
        functools.partial(_in_kernel, kv_only=kv_only),
        grid=(T // tm,),
        in_specs=[
            pl.BlockSpec((tm, D), lambda i: (i, 0)),
            pl.BlockSpec((1, D), lambda i: (0, 0)),
            pl.BlockSpec((None, 1, D), lambda i: (i // tiles_per_mod, 0, 0)),
            pl.BlockSpec((None, 1, D), lambda i: (i // tiles_per_mod, 0, 0)),
            pl.BlockSpec((None, D, IN_WIDTH), lambda i: (layer, 0, 0), pipeline_mode=pl.Buffered(1)),
            pl.BlockSpec((tm, HEAD_DIM), lambda i: (i % tiles_per_seq, 0)),
            pl.BlockSpec((tm, HEAD_DIM), lambda i: (i % tiles_per_seq, 0)),
            pl.BlockSpec((tm, HEAD_DIM), lambda i: (i % tiles_per_seq, 0)),
        ],
        out_specs=pl.BlockSpec((tm, IN_WIDTH), lambda i: (i, 0)),
        out_shape=jax.ShapeDtypeStruct((T, IN_WIDTH), _BF16),
        name="in_proj",
        compiler_params=_params("parallel"),
    )(x2d, pre_g.reshape(1, D), scale, shift, w_bf16, cos, sa, sb)


def _softmax_pv(qq, sources):
    scores = []
    for k, _, bias in sources:
        s = _dot_nt(qq, k)
        if bias is not None:
            s = s + bias
        scores.append(s)
    m = jnp.max(scores[0], axis=-1, keepdims=True)
    for s in scores[1:]:
        m = jnp.maximum(m, jnp.max(s, axis=-1, keepdims=True))
    acc = None
    for s, (_, v, _) in zip(scores, sources):
        p = jnp.exp2(s - m).astype(_BF16)
        v_aug = jnp.concatenate([v, jnp.ones(v.shape, _BF16)], axis=1)
        pv = jnp.dot(p, v_aug, preferred_element_type=_F32)
        acc = pv if acc is None else acc + pv
    return acc[:, :HEAD_DIM] / acc[:, HEAD_DIM:]


def _branch_out(o, g, factor, z_ref, o_ref):
    z = z_ref[...].astype(_F32)
    o_ref[...] = (_rms(o, g) * factor * _silu(z)).astype(_BF16)


def _da_kernel(*refs, n_src, lam_init):
    q_ref, z_ref = refs[0], refs[1]
    kv_refs = refs[2:2 + 2 * n_src]
    lam_ref, g_ref, o_ref, s_ref = refs[2 + 2 * n_src:]
    tq = DA_Q_TILE
    n_tiles = q_ref.shape[0] // tq

    chunks = []
    col = 0
    for i in range(n_src):
        k_ref, v_ref = kv_refs[2 * i], kv_refs[2 * i + 1]
        size = k_ref.shape[0]
        ck = min(size, DA_KEY_CHUNK)
        for c in range(size // ck):
            chunks.append((k_ref, v_ref, c * ck, ck, col))
            col += ck

    lp = lam_ref[...]
    lam = (jnp.exp(jnp.sum(lp[0:1] * lp[1:2], axis=-1, keepdims=True))
           - jnp.exp(jnp.sum(lp[2:3] * lp[3:4], axis=-1, keepdims=True)) + lam_init)

    def scores_and_max(t):
        q = q_ref[t * tq:(t + 1) * tq, :]
        lane = lax.broadcasted_iota(jnp.int32, q.shape, 1)
        zero = jnp.zeros_like(q)
        qq = jnp.concatenate([jnp.where(lane < DA_QK_DIM, q, zero),
                              jnp.where(lane >= DA_QK_DIM, q, zero)], axis=0)
        mx = None
        for k_ref, _, r0, ck, c0 in chunks:
            s = _dot_nt(qq, k_ref[r0:r0 + ck, :])
            s_ref[t % 2, :, c0:c0 + ck] = s
            for jj in range(ck // HEAD_DIM):
                part = s[:, jj * HEAD_DIM:(jj + 1) * HEAD_DIM]
                mx = part if mx is None else jnp.maximum(mx, part)
        return jnp.max(mx, axis=-1, keepdims=True)

    def softmax_pv_out(t, m):
        acc = None
        for _, v_ref, r0, ck, c0 in chunks:
            p = jnp.exp2(s_ref[t % 2, :, c0:c0 + ck] - m).astype(_BF16)
            v_aug = jnp.concatenate([v_ref[r0:r0 + ck, :], jnp.ones((ck, HEAD_DIM), _BF16)], axis=1)
            pv = jnp.dot(p, v_aug, preferred_element_type=_F32)
            acc = pv if acc is None else acc + pv
        o = acc[:, :HEAD_DIM] / acc[:, HEAD_DIM:]
        oa = o[:tq] - lam * o[tq:]
        rows = slice(t * tq, (t + 1) * tq)
        z = z_ref[rows, :].astype(_F32)
        o_ref[rows, :] = (_rms(oa, g_ref[...]) * (1.0 - lam_init) * _silu(z)).astype(_BF16)

    row_max = scores_and_max(0)
    for t in range(1, n_tiles):
        next_max = scores_and_max(t)
        softmax_pv_out(t - 1, row_max)
        row_max = next_max
    softmax_pv_out(n_tiles - 1, row_max)


def _diff_attn(q_proj, kv_projs, lam_params, da_g, *, lam_init):
    B, Sq, _ = q_proj.shape
    tq = min(DA_TILES_PER_STEP * DA_Q_TILE, Sq)
    in_specs = [
        pl.BlockSpec((None, tq, HEAD_DIM), lambda b, h, i: (b, i, COL_AQ + h)),
        pl.BlockSpec((None, tq, HEAD_DIM), lambda b, h, i: (b, i, COL_AZ + h)),
    ]
    args = [q_proj, q_proj]
    for kp in kv_projs:
        sk = kp.shape[1]
        in_specs.append(pl.BlockSpec((None, sk, HEAD_DIM), lambda b, h, i: (b, 0, COL_AK + h)))
        in_specs.append(pl.BlockSpec((None, sk, HEAD_DIM), lambda b, h, i: (b, 0, COL_AV + h)))
        args += [kp, kp]
    in_specs += [
        pl.BlockSpec((4, DA_QK_DIM), lambda b, h, i: (0, 0)),
        pl.BlockSpec((1, HEAD_DIM), lambda b, h, i: (0, 0)),
    ]
    args += [lam_params, da_g.reshape(1, HEAD_DIM)]
    return pl.pallas_call(
        functools.partial(_da_kernel, n_src=len(kv_projs), lam_init=lam_init),
        grid=(B, DA_HEADS, Sq // tq),
        in_specs=in_specs,
        out_specs=pl.BlockSpec((None, tq, HEAD_DIM), lambda b, h, i: (b, i, h)),
        out_shape=jax.ShapeDtypeStruct((B, Sq, DA_WIDTH), _BF16),
        scratch_shapes=[pltpu.VMEM((2, 2 * DA_Q_TILE, sum(kp.shape[1] for kp in kv_projs)), _F32)],
        name="diff_attn",
        compiler_params=_params("parallel", "parallel", "parallel"),
    )(*args)


def _na_kernel(q_ref, z_ref, k_ref, v_ref, kc_ref, vc_ref, bias_ref, g_ref, o_ref):
    step = pl.program_id(1)
    rows = k_ref.shape[0] // GRID_W
    n_blocks = rows // NA_BLOCK_ROWS
    n_ctx = kc_ref.shape[0]
    ones = jnp.ones((NA_K, HEAD_DIM), _BF16)

    def key_start(j):
        rb = step * NA_BLOCKS_PER_STEP + j
        kr0 = jnp.clip(rb * NA_BLOCK_ROWS - NA_WIN_H // 2, 0, rows - NA_KEY_ROWS)
        return pl.multiple_of(kr0 * GRID_W, GRID_W)

    def bias_table(j):
        rb = step * NA_BLOCKS_PER_STEP + j
        return jnp.where(rb == 0, 0, jnp.where(rb == n_blocks - 1, 2, 1))

    def scores_and_max(j, h):
        cols = slice(h * HEAD_DIM, (h + 1) * HEAD_DIM)
        q = q_ref[j * NA_Q:(j + 1) * NA_Q, cols]
        s_loc = _dot_nt(q, k_ref[pl.ds(key_start(j), NA_K), cols]) + bias_ref[bias_table(j), h]
        s_ctx = _dot_nt(q, kc_ref[:, cols])
        m = jnp.maximum(jnp.max(s_loc, axis=-1, keepdims=True), jnp.max(s_ctx, axis=-1, keepdims=True))
        return s_loc, s_ctx, m

    def softmax_pv_out(j, h, s_loc, s_ctx, m):
        cols = slice(h * HEAD_DIM, (h + 1) * HEAD_DIM)
        v_loc = jnp.concatenate([v_ref[pl.ds(key_start(j), NA_K), cols], ones], axis=1)
        v_ctx = jnp.concatenate([vc_ref[:, cols], ones[:n_ctx]], axis=1)
        acc = jnp.dot(jnp.exp2(s_loc - m).astype(_BF16), v_loc, preferred_element_type=_F32)
        acc = acc + jnp.dot(jnp.exp2(s_ctx - m).astype(_BF16), v_ctx, preferred_element_type=_F32)
        o = acc[:, :HEAD_DIM] / acc[:, HEAD_DIM:]
        qrows = slice(j * NA_Q, (j + 1) * NA_Q)
        z = z_ref[qrows, cols].astype(_F32)
        o_ref[qrows, cols] = (_rms(o, g_ref[:, cols]) * _silu(z)).astype(_BF16)

    units = [(j, h) for j in range(NA_BLOCKS_PER_STEP) for h in range(NA_HEADS)]
    pending = scores_and_max(*units[0])
    for u in range(1, len(units)):
        nxt = scores_and_max(*units[u])
        softmax_pv_out(*units[u - 1], *pending)
        pending = nxt
    softmax_pv_out(*units[-1], *pending)


def _na_bias_kernel(r_ref, o_ref, *, rows):
    shape = (GRID_W, 2 * GRID_W)
    c = lax.broadcasted_iota(jnp.int32, shape, 0)
    lane = lax.broadcasted_iota(jnp.int32, shape, 1)
    kc = lane & (GRID_W - 1)
    cs = jnp.clip(c - NA_WIN_W // 2, 0, GRID_W - NA_WIN_W)
    col_valid = (kc >= cs) & (kc < cs + NA_WIN_W)
    first_key_row = lane < GRID_W
    toeplitz = []
    for t in range(2 * NA_WIN_H):
        row = jnp.broadcast_to(r_ref[t:t + 1, :], shape)
        x = pltpu.roll(row, 2 * GRID_W - (NA_WIN_W - 1), 1, stride=1, stride_axis=0)
        toeplitz.append(jnp.where(col_valid, x * LOG2E, MASK_VALUE))
    masked = jnp.full(shape, MASK_VALUE, _F32)
    for cfg, (r0, kr0) in enumerate(_na_block_configs(rows)):
        for ri in range(NA_BLOCK_ROWS):
            r = r0 + ri
            rs = min(max(r - NA_WIN_H // 2, 0), rows - NA_WIN_H)
            for j in range(NA_KEY_ROWS // 2):
                kra = kr0 + 2 * j
                valid_a = rs <= kra < rs + NA_WIN_H
                valid_b = rs <= kra + 1 < rs + NA_WIN_H
                t = kra - r + NA_WIN_H
                if valid_a and valid_b:
                    blk = toeplitz[t]
                elif valid_a:
                    blk = jnp.where(first_key_row, toeplitz[t], MASK_VALUE)
                elif valid_b:
                    blk = jnp.where(first_key_row, MASK_VALUE, toeplitz[t])
                else:
                    blk = masked
                o_ref[cfg, ri * GRID_W:(ri + 1) * GRID_W, j * 2 * GRID_W:(j + 1) * 2 * GRID_W] = blk


def _na_block_configs(rows):
    return ((0, 0), (2 * NA_BLOCK_ROWS, 2 * NA_BLOCK_ROWS - NA_WIN_H // 2),
            (rows - NA_BLOCK_ROWS, rows - NA_KEY_ROWS))


def _na_bias_tables(rpb, rows):
    H, nr, ncol = rpb.shape
    lo = jnp.pad(rpb, ((0, 0), (1, 0), (0, 2 * GRID_W - ncol)))
    hi = jnp.pad(rpb, ((0, 0), (0, 1), (GRID_W, GRID_W - ncol)))
    return pl.pallas_call(
        functools.partial(_na_bias_kernel, rows=rows),
        grid=(H,),
        in_specs=[pl.BlockSpec((None, nr + 1, 2 * GRID_W), lambda h: (h, 0, 0))],
        out_specs=pl.BlockSpec((3, None, NA_Q, NA_K), lambda h: (0, h, 0, 0)),
        out_shape=jax.ShapeDtypeStruct((3, H, NA_Q, NA_K), _F32),
        name="na_bias",
        compiler_params=_params("parallel"),
    )(lo + hi)


def _nbr_attn(proj, proj_ctx, bias_tabs, na_g):
    B, S, _ = proj.shape
    C = proj_ctx.shape[1]
    n_blocks = S // NA_Q
    step_q = NA_BLOCKS_PER_STEP * NA_Q

    cq, ck, cv, cz = (c * HEAD_DIM // NA_WIDTH for c in (COL_BQ, COL_BK, COL_BV, COL_BZ))
    return pl.pallas_call(
        _na_kernel,
        grid=(B, n_blocks // NA_BLOCKS_PER_STEP),
        in_specs=[
            pl.BlockSpec((None, step_q, NA_WIDTH), lambda b, i: (b, i, cq)),
            pl.BlockSpec((None, step_q, NA_WIDTH), lambda b, i: (b, i, cz)),
            pl.BlockSpec((None, S, NA_WIDTH), lambda b, i: (b, 0, ck)),
            pl.BlockSpec((None, S, NA_WIDTH), lambda b, i: (b, 0, cv)),
            pl.BlockSpec((None, C, NA_WIDTH), lambda b, i: (b, 0, ck)),
            pl.BlockSpec((None, C, NA_WIDTH), lambda b, i: (b, 0, cv)),
            pl.BlockSpec(bias_tabs.shape, lambda b, i: (0, 0, 0, 0), pipeline_mode=pl.Buffered(1)),
            pl.BlockSpec((1, NA_WIDTH), lambda b, i: (0, 0)),
        ],
        out_specs=pl.BlockSpec((None, step_q, NA_WIDTH), lambda b, i: (b, i, 0)),
        out_shape=jax.ShapeDtypeStruct((B, S, NA_WIDTH), _BF16),
        name="nbr_attn",
        compiler_params=_params("parallel", "parallel"),
    )(proj, proj, proj, proj, proj_ctx, proj_ctx, bias_tabs, na_g.reshape(1, NA_WIDTH))


def _ctx_attn_kernel(q_ref, z_ref, k_ref, v_ref, g_ref, o_ref):
    o = _softmax_pv(q_ref[...], [(k_ref[...], v_ref[...], None)])
    _branch_out(o, g_ref[...], 1.0, z_ref, o_ref)


def _ctx_attn(proj_ctx, na_g):
    B, C, _ = proj_ctx.shape
    return pl.pallas_call(
        _ctx_attn_kernel,
        grid=(B, NA_HEADS),
        in_specs=[
            pl.BlockSpec((None, C, HEAD_DIM), lambda b, h: (b, 0, COL_BQ + h)),
            pl.BlockSpec((None, C, HEAD_DIM), lambda b, h: (b, 0, COL_BZ + h)),
            pl.BlockSpec((None, C, HEAD_DIM), lambda b, h: (b, 0, COL_BK + h)),
            pl.BlockSpec((None, C, HEAD_DIM), lambda b, h: (b, 0, COL_BV + h)),
            pl.BlockSpec((None, 1, HEAD_DIM), lambda b, h: (h, 0, 0)),
        ],
        out_specs=pl.BlockSpec((None, C, HEAD_DIM), lambda b, h: (b, 0, h)),
        out_shape=jax.ShapeDtypeStruct((B, C, NA_WIDTH), _BF16),
        name="ctx_attn",
        compiler_params=_params("parallel", "parallel"),
    )(proj_ctx, proj_ctx, proj_ctx, proj_ctx, na_g.reshape(NA_HEADS, 1, HEAD_DIM))


def _gmlp_rows(rows, u_ref, v_ref, z_ref, lng_ref, lnb_ref, ws_ref, bs_ref, g_ref):
    n_chunks = (rows.stop - rows.start) // CHUNK
    groups = []
    for g in range(MLP_GROUPS):
        cols = slice(g * HEAD_DIM, (g + 1) * HEAD_DIM)
        w = ws_ref[g].astype(_BF16)
        bs = jnp.concatenate([bs_ref[g]] * n_chunks, axis=0)
        v = v_ref[rows, cols].astype(_F32)
        vc = v - jnp.mean(v, axis=-1, keepdims=True)
        vn = vc * lax.rsqrt(jnp.mean(vc * vc, axis=-1, keepdims=True) + EPS) * lng_ref[:, cols] + lnb_ref[:, cols]
        vn = vn.astype(_BF16)
        mixed = jnp.concatenate(
            [jnp.dot(w, vn[c * CHUNK:(c + 1) * CHUNK, :], preferred_element_type=_F32) for c in range(n_chunks)],
            axis=0) + bs
        om = u_ref[rows, cols].astype(_F32) * mixed
        z = z_ref[rows, cols].astype(_F32)
        groups.append((_rms(om, g_ref[:, cols]) * _silu(z)).astype(_BF16))
    return jnp.concatenate(groups, axis=1)


def _out_kernel(ya_ref, yn_ref, u_ref, v_ref, z_ref, lng_ref, lnb_ref, ws_ref, bs_ref, mg_ref,
                w_ref, x_ref, gate_ref, g_ref, o_ref):
    sub_tiles = [slice(r0, r0 + OUT_SUB_ROWS) for r0 in range(0, x_ref.shape[0], OUT_SUB_ROWS)]
    yms = [_gmlp_rows(rows, u_ref, v_ref, z_ref, lng_ref, lnb_ref, ws_ref, bs_ref, mg_ref) for rows in sub_tiles]
    for rows, ym in zip(sub_tiles, yms):
        y = jnp.dot(ya_ref[rows, :], w_ref[:DA_WIDTH, :], preferred_element_type=_F32)
        y = y + jnp.dot(yn_ref[rows, :], w_ref[DA_WIDTH:DA_WIDTH + NA_WIDTH, :], preferred_element_type=_F32)
        y = y + jnp.dot(ym, w_ref[DA_WIDTH + NA_WIDTH:, :], preferred_element_type=_F32)
        o_ref[rows, :] = x_ref[rows, :] + gate_ref[...] * _rms(y, g_ref[...])


def _out_proj(ya, yn, proj, mlp_params, w_bf16, layer, x2d, gate, post_g):
    T, D = x2d.shape
    nb = gate.shape[0]
    tm = min(OUT_TILE_ROWS, T // nb)
    tiles_per_mod = (T // nb) // tm
    ln_g, ln_b, ws, bs, out_g = mlp_params
    proj2d = proj.reshape(T, IN_WIDTH)
    vec = pl.BlockSpec((1, MLP_WIDTH), lambda i: (0, 0))
    return pl.pallas_call(
        _out_kernel,
        grid=(T // tm,),
        in_specs=[
            pl.BlockSpec((tm, DA_WIDTH), lambda i: (i, 0)),
            pl.BlockSpec((tm, NA_WIDTH), lambda i: (i, 0)),
            pl.BlockSpec((tm, MLP_WIDTH), lambda i: (i, COL_CU)),
            pl.BlockSpec((tm, MLP_WIDTH), lambda i: (i, COL_CV)),
            pl.BlockSpec((tm, MLP_WIDTH), lambda i: (i, COL_CZ)),
            vec,
            vec,
            pl.BlockSpec((MLP_GROUPS, CHUNK, CHUNK), lambda i: (0, 0, 0)),
            pl.BlockSpec((MLP_GROUPS, CHUNK, 1), lambda i: (0, 0, 0)),
            vec,
            pl.BlockSpec((None, MIX_WIDTH, D), lambda i: (layer, 0, 0), pipeline_mode=pl.Buffered(1)),
            pl.BlockSpec((tm, D), lambda i: (i, 0)),
            pl.BlockSpec((None, 1, D), lambda i: (i // tiles_per_mod, 0, 0)),
            pl.BlockSpec((1, D), lambda i: (0, 0)),
        ],
        out_specs=pl.BlockSpec((tm, D), lambda i: (i, 0)),
        out_shape=jax.ShapeDtypeStruct((T, D), _F32),
        name="out_proj",
        compiler_params=_params("parallel"),
    )(ya.reshape(T, DA_WIDTH), yn.reshape(T, NA_WIDTH), proj2d, proj2d, proj2d,
      ln_g.reshape(1, MLP_WIDTH), ln_b.reshape(1, MLP_WIDTH), ws, bs.reshape(MLP_GROUPS, CHUNK, 1),
      out_g.reshape(1, MLP_WIDTH), w_bf16, x2d, gate, post_g.reshape(1, D))


def _rope_tables(seq):
    n = DA_QK_DIM // 4
    freqs = ROPE_BASE ** (-np.arange(n, dtype=np.float64) / n)
    pos = np.arange(seq)
    row = (pos // GRID_W).astype(np.float64)[:, None] * freqs
    col = (pos % GRID_W).astype(np.float64)[:, None] * freqs
    zeros = np.zeros((seq, n))

    def one_map(fn_first, fn_second):
        return np.concatenate([fn_first(row), fn_second(row), fn_first(col), fn_second(col)], axis=-1)

    cos = one_map(np.cos, np.cos)
    sa = one_map(lambda a: -np.sin(a), lambda a: zeros)
    sb = one_map(lambda a: zeros, np.sin)
    return tuple(jnp.asarray(np.concatenate([t, t], axis=-1), _F32) for t in (cos, sa, sb))


def _identity_rope_tables(seq):
    return (jnp.ones((seq, HEAD_DIM), _F32), jnp.zeros((seq, HEAD_DIM), _F32),
            jnp.zeros((seq, HEAD_DIM), _F32))


def kernel(x, c, ctx, c_ctx, ada_w, ada_b, pre_g, post_g, w_in, w_out, lam_q1, lam_k1, lam_q2, lam_k2,
           da_g, na_rpb, na_g, mlp_ln_g, mlp_ln_b, mlp_ws, mlp_bs, mlp_g):
    B, S, D = x.shape
    C = ctx.shape[1]
    L = ada_w.shape[0]
    assert S % (NA_BLOCKS_PER_STEP * NA_Q) == 0 and S // GRID_W >= NA_KEY_ROWS and C % CHUNK == 0

    assert B + 1 <= ADA_ROWS
    cc = jnp.concatenate([c, c_ctx[None], jnp.zeros((ADA_ROWS - B - 1, D), _F32)], axis=0)
    mod = _ada(cc, ada_w, ada_b)

    rope = _rope_tables(S)
    rope_id = _identity_rope_tables(C)
    w_in_b = w_in.astype(_BF16)
    w_out_b = w_out.astype(_BF16)
    x2d = x.reshape(B * S, D)
    xc2d = ctx.reshape(B * C, D)

    for l in range(L):
        lam_init = 0.8 - 0.6 * math.exp(-0.3 * l)
        ctx_out = l < L - 1
        shift, scale, gate = (mod[l, :B, i * D:(i + 1) * D].reshape(B, 1, D) for i in range(3))
        shift_c, scale_c, gate_c = (mod[l, B:B + 1, i * D:(i + 1) * D].reshape(1, 1, D) for i in range(3))
        lam_params = jnp.stack([lam_q1[l], lam_k1[l], lam_q2[l], lam_k2[l]])

        proj = _in_proj(x2d, pre_g[l], scale, shift, w_in_b, l, *rope, seq=S).reshape(B, S, IN_WIDTH)
        proj_c = _in_proj(xc2d, pre_g[l], scale_c, shift_c, w_in_b, l, *rope_id, seq=C,
                          kv_only=not ctx_out).reshape(B, C, IN_WIDTH)

        ya = _diff_attn(proj, [proj, proj_c], lam_params, da_g[l], lam_init=lam_init)
        yn = _nbr_attn(proj, proj_c, _na_bias_tables(na_rpb[l], S // GRID_W), na_g[l])
        mlp_params = (mlp_ln_g[l], mlp_ln_b[l], mlp_ws[l], mlp_bs[l], mlp_g[l])
        x2d_new = _out_proj(ya, yn, proj, mlp_params, w_out_b, l, x2d, gate, post_g[l])

        if ctx_out:
            yca = _diff_attn(proj_c, [proj_c], lam_params, da_g[l], lam_init=lam_init)
            ycn = _ctx_attn(proj_c, na_g[l])
            xc2d = _out_proj(yca, ycn, proj_c, mlp_params, w_out_b, l, xc2d, gate_c, post_g[l])
        x2d = x2d_new

    return x2d.reshape(B, S, D)
```

```python
import functools
import math

import numpy as np
import jax
import jax.numpy as jnp
from jax import lax
from jax.experimental import pallas as pl
from jax.experimental.pallas import tpu as pltpu

D_MODEL = 2048
DEPTH = 2
GRID_W = 64
HEAD_DIM = 128
DA_HEADS = 6
NA_HEADS = 6
MLP_GROUPS = 4
DA_WIDTH = DA_HEADS * HEAD_DIM
NA_WIDTH = NA_HEADS * HEAD_DIM
MLP_WIDTH = MLP_GROUPS * HEAD_DIM
DA_QK_DIM = HEAD_DIM // 2
NA_WIN_H = 8
NA_WIN_W = 16
CHUNK = 128
ROPE_BASE = 10000.0
EPS = 1e-6
IN_WIDTH = 4 * DA_WIDTH + 4 * NA_WIDTH + 3 * MLP_WIDTH
MIX_WIDTH = DA_WIDTH + NA_WIDTH + MLP_WIDTH

COL_AQ, COL_AK, COL_AV, COL_AZ = 0, 6, 12, 18
COL_BQ, COL_BK, COL_BV, COL_BZ = 24, 30, 36, 42
COL_CU, COL_CV, COL_CZ = 12, 13, 14

ROPE_PAIR_DIST = DA_QK_DIM // 4
LOG2E = math.log2(math.e)
DA_SCALE = DA_QK_DIM ** -0.5 * LOG2E
NA_SCALE = HEAD_DIM ** -0.5 * LOG2E
DA_KEY_CHUNK = 512
DA_Q_TILE = 128
DA_TILES_PER_STEP = 16


def _in_segments():
    kinds = ["rope_scale", "rope", "plain", "plain", "na_scale", "plain", "plain", "plain", "gelu", "gelu", "plain"]
    widths = [DA_WIDTH] * 4 + [NA_WIDTH] * 4 + [MLP_WIDTH] * 3
    segs, c0 = [], 0
    for width, kind in zip(widths, kinds):
        segs.append((c0, width, kind))
        c0 += width
    return tuple(segs)


ADA_TILE_COLS = 768
ADA_ROWS = 8
IN_SEGMENTS = _in_segments()
IN_KV_SEGMENTS = (1, 2, 5, 6)
IN_TILE_ROWS = 256
OUT_TILE_ROWS = 512
OUT_SUB_ROWS = 256

NA_BLOCK_ROWS = 4
NA_KEY_ROWS = NA_BLOCK_ROWS + NA_WIN_H
NA_BLOCKS_PER_STEP = 2
NA_Q = NA_BLOCK_ROWS * GRID_W
NA_K = NA_KEY_ROWS * GRID_W
MASK_VALUE = -1e30

VMEM_LIMIT_BYTES = 56 * 1024 * 1024

_BF16 = jnp.bfloat16
_F32 = jnp.float32


def _silu(x):
    return x * (1.0 / (1.0 + jnp.exp(-x)))


def _gelu_tanh(x):
    c = math.sqrt(2.0 / math.pi)
    return 0.5 * x * (1.0 + jnp.tanh(c * (x + 0.044715 * (x * x * x))))


def _rms(x, g):
    return x * lax.rsqrt(jnp.mean(x * x, axis=-1, keepdims=True) + EPS) * g


def _dot_nt(a, b):
    return lax.dot_general(a, b, (((1,), (1,)), ((), ())), preferred_element_type=_F32)


def _params(*sem):
    return pltpu.CompilerParams(dimension_semantics=sem, vmem_limit_bytes=VMEM_LIMIT_BYTES)


def _ada_kernel(c_ref, w_ref, b_ref, o_ref):
    sc = _silu(c_ref[...]).astype(_BF16)
    o_ref[...] = jnp.dot(sc, w_ref[...].astype(_BF16), preferred_element_type=_F32) + b_ref[...]


def _ada(cc, ada_w, ada_b):
    L, D, N = ada_w.shape
    R = cc.shape[0]
    tn = ADA_TILE_COLS
    return pl.pallas_call(
        _ada_kernel,
        grid=(L, N // tn),
        in_specs=[
            pl.BlockSpec((R, D), lambda l, j: (0, 0)),
            pl.BlockSpec((None, D, tn), lambda l, j: (l, 0, j)),
            pl.BlockSpec((None, 1, tn), lambda l, j: (l, 0, j)),
        ],
        out_specs=pl.BlockSpec((None, R, tn), lambda l, j: (l, 0, j)),
        out_shape=jax.ShapeDtypeStruct((L, R, N), _F32),
        name="ada_mod",
        compiler_params=_params("parallel", "parallel"),
    )(cc, ada_w, ada_b.reshape(L, 1, N))


def _in_kernel(x_ref, g_ref, scale_ref, shift_ref, w_ref, cos_ref, sa_ref, sb_ref, o_ref, *, kv_only):
    x = x_ref[...]
    h = (_rms(x, g_ref[...]) * (1.0 + scale_ref[...]) + shift_ref[...]).astype(_BF16)
    cos, sa, sb = cos_ref[...], sa_ref[...], sb_ref[...]
    for idx, (c0, width, epilogue) in enumerate(IN_SEGMENTS):
        if kv_only and idx not in IN_KV_SEGMENTS:
            o_ref[:, c0:c0 + width] = jnp.zeros((o_ref.shape[0], width), _BF16)
            continue
        acc = jnp.dot(h, w_ref[:, c0:c0 + width], preferred_element_type=_F32)
        if epilogue in ("rope", "rope_scale"):
            for hh in range(width // HEAD_DIM):
                xs = acc[:, hh * HEAD_DIM:(hh + 1) * HEAD_DIM]
                r = (xs * cos + pltpu.roll(xs, HEAD_DIM - ROPE_PAIR_DIST, 1) * sa
                     + pltpu.roll(xs, ROPE_PAIR_DIST, 1) * sb)
                if epilogue == "rope_scale":
                    r = r * DA_SCALE
                o_ref[:, c0 + hh * HEAD_DIM:c0 + (hh + 1) * HEAD_DIM] = r.astype(_BF16)
        elif epilogue == "na_scale":
            o_ref[:, c0:c0 + width] = (acc * NA_SCALE).astype(_BF16)
        elif epilogue == "gelu":
            o_ref[:, c0:c0 + width] = _gelu_tanh(acc).astype(_BF16)
        else:
            o_ref[:, c0:c0 + width] = acc.astype(_BF16)


def _in_proj(x2d, pre_g, scale, shift, w_bf16, layer, cos, sa, sb, *, seq, kv_only=False):
    T, D = x2d.shape
    nb = scale.shape[0]
    tm = IN_TILE_ROWS
    tiles_per_mod = (T // nb) // tm
    tiles_per_seq = seq // tm
    return pl.pallas_call(
        functools.partial(_in_kernel, kv_only=kv_only),
        grid=(T // tm,),
        in_specs=[
            pl.BlockSpec((tm, D), lambda i: (i, 0)),
            pl.BlockSpec((1, D), lambda i: (0, 0)),
            pl.BlockSpec((None, 1, D), lambda i: (i // tiles_per_mod, 0, 0)),
            pl.BlockSpec((None, 1, D), lambda i: (i // tiles_per_mod, 0, 0)),
            pl.BlockSpec((None, D, IN_WIDTH), lambda i: (layer, 0, 0), pipeline_mode=pl.Buffered(1)),
            pl.BlockSpec((tm, HEAD_DIM), lambda i: (i % tiles_per_seq, 0)),
            pl.BlockSpec((tm, HEAD_DIM), lambda i: (i % tiles_per_seq, 0)),
            pl.BlockSpec((tm, HEAD_DIM), lambda i: (i % tiles_per_seq, 0)),
        ],
        out_specs=pl.BlockSpec((tm, IN_WIDTH), lambda i: (i, 0)),
        out_shape=jax.ShapeDtypeStruct((T, IN_WIDTH), _BF16),
        name="in_proj",
        compiler_params=_params("parallel"),
    )(x2d, pre_g.reshape(1, D), scale, shift, w_bf16, cos, sa, sb)


def _softmax_pv(qq, sources):
    scores = []
    for k, _, bias in sources:
        s = _dot_nt(qq, k)
        if bias is not None:
            s = s + bias
        scores.append(s)
    m = jnp.max(scores[0], axis=-1, keepdims=True)
    for s in scores[1:]:
        m = jnp.maximum(m, jnp.max(s, axis=-1, keepdims=True))
    acc = None
    for s, (_, v, _) in zip(scores, sources):
        p = jnp.exp2(s - m).astype(_BF16)
        v_aug = jnp.concatenate([v, jnp.ones(v.shape, _BF16)], axis=1)
        pv = jnp.dot(p, v_aug, preferred_element_type=_F32)
        acc = pv if acc is None else acc + pv
    return acc[:, :HEAD_DIM] / acc[:, HEAD_DIM:]


def _da_kernel(*refs, n_src, lam_init):
    q_ref, z_ref = refs[0], refs[1]
    kv_refs = refs[2:2 + 2 * n_src]
    lam_ref, g_ref, o_ref, s_ref = refs[2 + 2 * n_src:]
    tq = DA_Q_TILE
    n_tiles = q_ref.shape[0] // tq

    chunks = []
    col = 0
    for i in range(n_src):
        k_ref, v_ref = kv_refs[2 * i], kv_refs[2 * i + 1]
        size = k_ref.shape[0]
        ck = min(size, DA_KEY_CHUNK)
        for c in range(size // ck):
            chunks.append((k_ref, v_ref, c * ck, ck, col))
            col += ck

    lp = lam_ref[...]
    lam = (jnp.exp(jnp.sum(lp[0:1] * lp[1:2], axis=-1, keepdims=True))
           - jnp.exp(jnp.sum(lp[2:3] * lp[3:4], axis=-1, keepdims=True)) + lam_init)

    def scores_and_max(t):
        q = q_ref[t * tq:(t + 1) * tq, :]
        lane = lax.broadcasted_iota(jnp.int32, q.shape, 1)
        zero = jnp.zeros_like(q)
        qq = jnp.concatenate([jnp.where(lane < DA_QK_DIM, q, zero),
                              jnp.where(lane >= DA_QK_DIM, q, zero)], axis=0)
        mx = None
        for k_ref, _, r0, ck, c0 in chunks:
            s = _dot_nt(qq, k_ref[r0:r0 + ck, :])
            s_ref[t % 2, :, c0:c0 + ck] = s
            for jj in range(ck // HEAD_DIM):
                part = s[:, jj * HEAD_DIM:(jj + 1) * HEAD_DIM]
                mx = part if mx is None else jnp.maximum(mx, part)
        return jnp.max(mx, axis=-1, keepdims=True)

    def softmax_pv_out(t, m):
        acc = None
        for _, v_ref, r0, ck, c0 in chunks:
            p = jnp.exp2(s_ref[t % 2, :, c0:c0 + ck] - m).astype(_BF16)
            v_aug = jnp.concatenate([v_ref[r0:r0 + ck, :], jnp.ones((ck, HEAD_DIM), _BF16)], axis=1)
            pv = jnp.dot(p, v_aug, preferred_element_type=_F32)
            acc = pv if acc is None else acc + pv
        o = acc[:, :HEAD_DIM] / acc[:, HEAD_DIM:]
        oa = o[:tq] - lam * o[tq:]
        rows = slice(t * tq, (t + 1) * tq)
        z = z_ref[rows, :].astype(_F32)
        o_ref[rows, :] = (_rms(oa, g_ref[...]) * (1.0 - lam_init) * _silu(z)).astype(_BF16)

    row_max = scores_and_max(0)
    for t in range(1, n_tiles):
        next_max = scores_and_max(t)
        softmax_pv_out(t - 1, row_max)
        row_max = next_max
    softmax_pv_out(n_tiles - 1, row_max)


def _diff_attn(q_proj, kv_projs, lam_params, da_g, *, lam_init):
    B, Sq, _ = q_proj.shape
    tq = min(DA_TILES_PER_STEP * DA_Q_TILE, Sq)
    in_specs = [
        pl.BlockSpec((None, tq, HEAD_DIM), lambda b, h, i: (b, i, COL_AQ + h)),
        pl.BlockSpec((None, tq, HEAD_DIM), lambda b, h, i: (b, i, COL_AZ + h)),
    ]
    args = [q_proj, q_proj]
    for kp in kv_projs:
        sk = kp.shape[1]
        in_specs.append(pl.BlockSpec((None, sk, HEAD_DIM), lambda b, h, i: (b, 0, COL_AK + h)))
        in_specs.append(pl.BlockSpec((None, sk, HEAD_DIM), lambda b, h, i: (b, 0, COL_AV + h)))
        args += [kp, kp]
    in_specs += [
        pl.BlockSpec((4, DA_QK_DIM), lambda b, h, i: (0, 0)),
        pl.BlockSpec((1, HEAD_DIM), lambda b, h, i: (0, 0)),
    ]
    args += [lam_params, da_g.reshape(1, HEAD_DIM)]
    return pl.pallas_call(
        functools.partial(_da_kernel, n_src=len(kv_projs), lam_init=lam_init),
        grid=(B, DA_HEADS, Sq // tq),
        in_specs=in_specs,
        out_specs=pl.BlockSpec((None, tq, HEAD_DIM), lambda b, h, i: (b, i, h)),
        out_shape=jax.ShapeDtypeStruct((B, Sq, DA_WIDTH), _BF16),
        scratch_shapes=[pltpu.VMEM((2, 2 * DA_Q_TILE, sum(kp.shape[1] for kp in kv_projs)), _F32)],
        name="diff_attn",
        compiler_params=_params("parallel", "parallel", "parallel"),
    )(*args)


def _na_kernel(q_ref, z_ref, k_ref, v_ref, kc_ref, vc_ref, bias_ref, g_ref, o_ref):
    step = pl.program_id(1)
    rows = k_ref.shape[0] // GRID_W
    n_blocks = rows // NA_BLOCK_ROWS
    n_ctx = kc_ref.shape[0]
    ones = jnp.ones((NA_K, HEAD_DIM), _BF16)

    def key_start(j):
        rb = step * NA_BLOCKS_PER_STEP + j
        kr0 = jnp.clip(rb * NA_BLOCK_ROWS - NA_WIN_H // 2, 0, rows - NA_KEY_ROWS)
        return pl.multiple_of(kr0 * GRID_W, GRID_W)

    def bias_table(j):
        rb = step * NA_BLOCKS_PER_STEP + j
        return jnp.where(rb == 0, 0, jnp.where(rb == n_blocks - 1, 2, 1))

    def scores_and_max(j, h):
        cols = slice(h * HEAD_DIM, (h + 1) * HEAD_DIM)
        q = q_ref[j * NA_Q:(j + 1) * NA_Q, cols]
        s_loc = _dot_nt(q, k_ref[pl.ds(key_start(j), NA_K), cols]) + bias_ref[bias_table(j), h]
        s_ctx = _dot_nt(q, kc_ref[:, cols])
        m = jnp.maximum(jnp.max(s_loc, axis=-1, keepdims=True), jnp.max(s_ctx, axis=-1, keepdims=True))
        return s_loc, s_ctx, m

    def softmax_pv_out(j, h, s_loc, s_ctx, m):
        cols = slice(h * HEAD_DIM, (h + 1) * HEAD_DIM)
        v_loc = jnp.concatenate([v_ref[pl.ds(key_start(j), NA_K), cols], ones], axis=1)
        v_ctx = jnp.concatenate([vc_ref[:, cols], ones[:n_ctx]], axis=1)
        acc = jnp.dot(jnp.exp2(s_loc - m).astype(_BF16), v_loc, preferred_element_type=_F32)
        acc = acc + jnp.dot(jnp.exp2(s_ctx - m).astype(_BF16), v_ctx, preferred_element_type=_F32)
        o = acc[:, :HEAD_DIM] / acc[:, HEAD_DIM:]
        qrows = slice(j * NA_Q, (j + 1) * NA_Q)
        z = z_ref[qrows, cols].astype(_F32)
        o_ref[qrows, cols] = (_rms(o, g_ref[:, cols]) * _silu(z)).astype(_BF16)

    units = [(j, h) for j in range(NA_BLOCKS_PER_STEP) for h in range(NA_HEADS)]
    pending = scores_and_max(*units[0])
    for u in range(1, len(units)):
        nxt = scores_and_max(*units[u])
        softmax_pv_out(*units[u - 1], *pending)
        pending = nxt
    softmax_pv_out(*units[-1], *pending)


def _na_bias_kernel(r_ref, o_ref, *, rows):
    shape = (GRID_W, 2 * GRID_W)
    c = lax.broadcasted_iota(jnp.int32, shape, 0)
    lane = lax.broadcasted_iota(jnp.int32, shape, 1)
    kc = lane & (GRID_W - 1)
    cs = jnp.clip(c - NA_WIN_W // 2, 0, GRID_W - NA_WIN_W)
    col_valid = (kc >= cs) & (kc < cs + NA_WIN_W)
    first_key_row = lane < GRID_W
    toeplitz = []
    for t in range(2 * NA_WIN_H):
        row = jnp.broadcast_to(r_ref[t:t + 1, :], shape)
        x = pltpu.roll(row, 2 * GRID_W - (NA_WIN_W - 1), 1, stride=1, stride_axis=0)
        toeplitz.append(jnp.where(col_valid, x * LOG2E, MASK_VALUE))
    masked = jnp.full(shape, MASK_VALUE, _F32)
    for cfg, (r0, kr0) in enumerate(_na_block_configs(rows)):
        for ri in range(NA_BLOCK_ROWS):
            r = r0 + ri
            rs = min(max(r - NA_WIN_H // 2, 0), rows - NA_WIN_H)
            for j in range(NA_KEY_ROWS // 2):
                kra = kr0 + 2 * j
                valid_a = rs <= kra < rs + NA_WIN_H
                valid_b = rs <= kra + 1 < rs + NA_WIN_H
                t = kra - r + NA_WIN_H
                if valid_a and valid_b:
                    blk = toeplitz[t]
                elif valid_a:
                    blk = jnp.where(first_key_row, toeplitz[t], MASK_VALUE)
                elif valid_b:
                    blk = jnp.where(first_key_row, MASK_VALUE, toeplitz[t])
                else:
                    blk = masked
                o_ref[cfg, ri * GRID_W:(ri + 1) * GRID_W, j * 2 * GRID_W:(j + 1) * 2 * GRID_W] = blk


def _na_block_configs(rows):
    return ((0, 0), (2 * NA_BLOCK_ROWS, 2 * NA_BLOCK_ROWS - NA_WIN_H // 2),
            (rows - NA_BLOCK_ROWS, rows - NA_KEY_ROWS))


def _na_bias_tables(rpb, rows):
    H, nr, ncol = rpb.shape
    lo = jnp.pad(rpb, ((0, 0), (1, 0), (0, 2 * GRID_W - ncol)))
    hi = jnp.pad(rpb, ((0, 0), (0, 1), (GRID_W, GRID_W - ncol)))
    return pl.pallas_call(
        functools.partial(_na_bias_kernel, rows=rows),
        grid=(H,),
        in_specs=[pl.BlockSpec((None, nr + 1, 2 * GRID_W), lambda h: (h, 0, 0))],
        out_specs=pl.BlockSpec((3, None, NA_Q, NA_K), lambda h: (0, h, 0, 0)),
        out_shape=jax.ShapeDtypeStruct((3, H, NA_Q, NA_K), _F32),
        name="na_bias",
        compiler_params=_params("parallel"),
    )(lo + hi)


def _nbr_attn(proj, proj_ctx, bias_tabs, na_g):
    B, S, _ = proj.shape
    C = proj_ctx.shape[1]
    n_blocks = S // NA_Q
    step_q = NA_BLOCKS_PER_STEP * NA_Q

    cq, ck, cv, cz = (c * HEAD_DIM // NA_WIDTH for c in (COL_BQ, COL_BK, COL_BV, COL_BZ))
    return pl.pallas_call(
        _na_kernel,
        grid=(B, n_blocks // NA_BLOCKS_PER_STEP),
        in_specs=[
            pl.BlockSpec((None, step_q, NA_WIDTH), lambda b, i: (b, i, cq)),
            pl.BlockSpec((None, step_q, NA_WIDTH), lambda b, i: (b, i, cz)),
            pl.BlockSpec((None, S, NA_WIDTH), lambda b, i: (b, 0, ck)),
            pl.BlockSpec((None, S, NA_WIDTH), lambda b, i: (b, 0, cv)),
            pl.BlockSpec((None, C, NA_WIDTH), lambda b, i: (b, 0, ck)),
            pl.BlockSpec((None, C, NA_WIDTH), lambda b, i: (b, 0, cv)),
            pl.BlockSpec(bias_tabs.shape, lambda b, i: (0, 0, 0, 0), pipeline_mode=pl.Buffered(1)),
            pl.BlockSpec((1, NA_WIDTH), lambda b, i: (0, 0)),
        ],
        out_specs=pl.BlockSpec((None, step_q, NA_WIDTH), lambda b, i: (b, i, 0)),
        out_shape=jax.ShapeDtypeStruct((B, S, NA_WIDTH), _BF16),
        name="nbr_attn",
        compiler_params=_params("parallel", "parallel"),
    )(proj, proj, proj, proj, proj_ctx, proj_ctx, bias_tabs, na_g.reshape(1, NA_WIDTH))


def _ctx_attn_kernel(q_ref, z_ref, k_ref, v_ref, g_ref, o_ref):
    for h in range(NA_HEADS):
        cols = slice(h * HEAD_DIM, (h + 1) * HEAD_DIM)
        o = _softmax_pv(q_ref[:, cols], [(k_ref[:, cols], v_ref[:, cols], None)])
        z = z_ref[:, cols].astype(_F32)
        o_ref[:, cols] = (_rms(o, g_ref[:, cols]) * _silu(z)).astype(_BF16)


def _ctx_attn(proj_ctx, na_g):
    B, C, _ = proj_ctx.shape
    cq, ck, cv, cz = (c * HEAD_DIM // NA_WIDTH for c in (COL_BQ, COL_BK, COL_BV, COL_BZ))
    return pl.pallas_call(
        _ctx_attn_kernel,
        grid=(B,),
        in_specs=[
            pl.BlockSpec((None, C, NA_WIDTH), lambda b: (b, 0, cq)),
            pl.BlockSpec((None, C, NA_WIDTH), lambda b: (b, 0, cz)),
            pl.BlockSpec((None, C, NA_WIDTH), lambda b: (b, 0, ck)),
            pl.BlockSpec((None, C, NA_WIDTH), lambda b: (b, 0, cv)),
            pl.BlockSpec((1, NA_WIDTH), lambda b: (0, 0)),
        ],
        out_specs=pl.BlockSpec((None, C, NA_WIDTH), lambda b: (b, 0, 0)),
        out_shape=jax.ShapeDtypeStruct((B, C, NA_WIDTH), _BF16),
        name="ctx_attn",
        compiler_params=_params("parallel"),
    )(proj_ctx, proj_ctx, proj_ctx, proj_ctx, na_g.reshape(1, NA_WIDTH))


def _gmlp_rows(rows, u_ref, v_ref, z_ref, lng_ref, lnb_ref, ws_ref, bs_ref, g_ref):
    n_chunks = (rows.stop - rows.start) // CHUNK
    groups = []
    for g in range(MLP_GROUPS):
        cols = slice(g * HEAD_DIM, (g + 1) * HEAD_DIM)
        w = ws_ref[g].astype(_BF16)
        bs = jnp.concatenate([bs_ref[g]] * n_chunks, axis=0)
        v = v_ref[rows, cols].astype(_F32)
        vc = v - jnp.mean(v, axis=-1, keepdims=True)
        vn = vc * lax.rsqrt(jnp.mean(vc * vc, axis=-1, keepdims=True) + EPS) * lng_ref[:, cols] + lnb_ref[:, cols]
        vn = vn.astype(_BF16)
        mixed = jnp.concatenate(
            [jnp.dot(w, vn[c * CHUNK:(c + 1) * CHUNK, :], preferred_element_type=_F32) for c in range(n_chunks)],
            axis=0) + bs
        om = u_ref[rows, cols].astype(_F32) * mixed
        z = z_ref[rows, cols].astype(_F32)
        groups.append((_rms(om, g_ref[:, cols]) * _silu(z)).astype(_BF16))
    return jnp.concatenate(groups, axis=1)


def _out_kernel(ya_ref, yn_ref, u_ref, v_ref, z_ref, lng_ref, lnb_ref, ws_ref, bs_ref, mg_ref,
                w_ref, x_ref, gate_ref, g_ref, o_ref):
    sub_tiles = [slice(r0, r0 + OUT_SUB_ROWS) for r0 in range(0, x_ref.shape[0], OUT_SUB_ROWS)]
    yms = [_gmlp_rows(rows, u_ref, v_ref, z_ref, lng_ref, lnb_ref, ws_ref, bs_ref, mg_ref) for rows in sub_tiles]
    for rows, ym in zip(sub_tiles, yms):
        y = jnp.dot(ya_ref[rows, :], w_ref[:DA_WIDTH, :], preferred_element_type=_F32)
        y = y + jnp.dot(yn_ref[rows, :], w_ref[DA_WIDTH:DA_WIDTH + NA_WIDTH, :], preferred_element_type=_F32)
        y = y + jnp.dot(ym, w_ref[DA_WIDTH + NA_WIDTH:, :], preferred_element_type=_F32)
        o_ref[rows, :] = x_ref[rows, :] + gate_ref[...] * _rms(y, g_ref[...])


def _out_proj(ya, yn, proj, mlp_params, w_bf16, layer, x2d, gate, post_g):
    T, D = x2d.shape
    nb = gate.shape[0]
    tm = min(OUT_TILE_ROWS, T // nb)
    tiles_per_mod = (T // nb) // tm
    ln_g, ln_b, ws, bs, out_g = mlp_params
    proj2d = proj.reshape(T, IN_WIDTH)
    vec = pl.BlockSpec((1, MLP_WIDTH), lambda i: (0, 0))
    return pl.pallas_call(
        _out_kernel,
        grid=(T // tm,),
        in_specs=[
            pl.BlockSpec((tm, DA_WIDTH), lambda i: (i, 0)),
            pl.BlockSpec((tm, NA_WIDTH), lambda i: (i, 0)),
            pl.BlockSpec((tm, MLP_WIDTH), lambda i: (i, COL_CU)),
            pl.BlockSpec((tm, MLP_WIDTH), lambda i: (i, COL_CV)),
            pl.BlockSpec((tm, MLP_WIDTH), lambda i: (i, COL_CZ)),
            vec,
            vec,
            pl.BlockSpec((MLP_GROUPS, CHUNK, CHUNK), lambda i: (0, 0, 0)),
            pl.BlockSpec((MLP_GROUPS, CHUNK, 1), lambda i: (0, 0, 0)),
            vec,
            pl.BlockSpec((None, MIX_WIDTH, D), lambda i: (layer, 0, 0), pipeline_mode=pl.Buffered(1)),
            pl.BlockSpec((tm, D), lambda i: (i, 0)),
            pl.BlockSpec((None, 1, D), lambda i: (i // tiles_per_mod, 0, 0)),
            pl.BlockSpec((1, D), lambda i: (0, 0)),
        ],
        out_specs=pl.BlockSpec((tm, D), lambda i: (i, 0)),
        out_shape=jax.ShapeDtypeStruct((T, D), _F32),
        name="out_proj",
        compiler_params=_params("parallel"),
    )(ya.reshape(T, DA_WIDTH), yn.reshape(T, NA_WIDTH), proj2d, proj2d, proj2d,
      ln_g.reshape(1, MLP_WIDTH), ln_b.reshape(1, MLP_WIDTH), ws, bs.reshape(MLP_GROUPS, CHUNK, 1),
      out_g.reshape(1, MLP_WIDTH), w_bf16, x2d, gate, post_g.reshape(1, D))


def _rope_tables(seq):
    n = DA_QK_DIM // 4
    freqs = ROPE_BASE ** (-np.arange(n, dtype=np.float64) / n)
    pos = np.arange(seq)
    row = (pos // GRID_W).astype(np.float64)[:, None] * freqs
    col = (pos % GRID_W).astype(np.float64)[:, None] * freqs
    zeros = np.zeros((seq, n))

    def one_map(fn_first, fn_second):
        return np.concatenate([fn_first(row), fn_second(row), fn_first(col), fn_second(col)], axis=-1)

    cos = one_map(np.cos, np.cos)
    sa = one_map(lambda a: -np.sin(a), lambda a: zeros)
    sb = one_map(lambda a: zeros, np.sin)
    return tuple(jnp.asarray(np.concatenate([t, t], axis=-1), _F32) for t in (cos, sa, sb))


def _identity_rope_tables(seq):
    return (jnp.ones((seq, HEAD_DIM), _F32), jnp.zeros((seq, HEAD_DIM), _F32),
            jnp.zeros((seq, HEAD_DIM), _F32))


def kernel(x, c, ctx, c_ctx, ada_w, ada_b, pre_g, post_g, w_in, w_out, lam_q1, lam_k1, lam_q2, lam_k2,
           da_g, na_rpb, na_g, mlp_ln_g, mlp_ln_b, mlp_ws, mlp_bs, mlp_g):
    B, S, D = x.shape
    C = ctx.shape[1]
    L = ada_w.shape[0]
    assert S % (NA_BLOCKS_PER_STEP * NA_Q) == 0 and S // GRID_W >= NA_KEY_ROWS and C % CHUNK == 0

    assert B + 1 <= ADA_ROWS
    cc = jnp.concatenate([c, c_ctx[None], jnp.zeros((ADA_ROWS - B - 1, D), _F32)], axis=0)
    mod = _ada(cc, ada_w, ada_b)

    rope = _rope_tables(S)
    rope_id = _identity_rope_tables(C)
    w_in_b = w_in.astype(_BF16)
    w_out_b = w_out.astype(_BF16)
    x2d = x.reshape(B * S, D)
    xc2d = ctx.reshape(B * C, D)

    for l in range(L):
        lam_init = 0.8 - 0.6 * math.exp(-0.3 * l)
        ctx_out = l < L - 1
        shift, scale, gate = (mod[l, :B, i * D:(i + 1) * D].reshape(B, 1, D) for i in range(3))
        shift_c, scale_c, gate_c = (mod[l, B:B + 1, i * D:(i + 1) * D].reshape(1, 1, D) for i in range(3))
        lam_params = jnp.stack([lam_q1[l], lam_k1[l], lam_q2[l], lam_k2[l]])

        proj = _in_proj(x2d, pre_g[l], scale, shift, w_in_b, l, *rope, seq=S).reshape(B, S, IN_WIDTH)
        proj_c = _in_proj(xc2d, pre_g[l], scale_c, shift_c, w_in_b, l, *rope_id, seq=C,
                          kv_only=not ctx_out).reshape(B, C, IN_WIDTH)

        ya = _diff_attn(proj, [proj, proj_c], lam_params, da_g[l], lam_init=lam_init)
        yn = _nbr_attn(proj, proj_c, _na_bias_tables(na_rpb[l], S // GRID_W), na_g[l])
        mlp_params = (mlp_ln_g[l], mlp_ln_b[l], mlp_ws[l], mlp_bs[l], mlp_g[l])
        x2d_new = _out_proj(ya, yn, proj, mlp_params, w_out_b, l, x2d, gate, post_g[l])

        if ctx_out:
            yca = _diff_attn(proj_c, [proj_c], lam_params, da_g[l], lam_init=lam_init)
            ycn = _ctx_attn(proj_c, na_g[l])
            xc2d = _out_proj(yca, ycn, proj_c, mlp_params, w_out_b, l, xc2d, gate_c, post_g[l])
        x2d = x2d_new

    return x2d.reshape(B, S, D)
```
